```python
import jax, jax.numpy as jnp
from jax import lax
import numpy as np

D_MODEL = 2048
BATCH = 16
SEQ = 2048
DEPTH = 1
DEC_BATCH = 2
DEC_SEQ = 4096
PAST_LEN = 128

MIX_WIDTH = D_MODEL
GMLP_WIDTH = MIX_WIDTH // 2
POOL_WIDTH = MIX_WIDTH - GMLP_WIDTH
GMLP_HEADS = 4
GMLP_HEAD_DIM = GMLP_WIDTH // GMLP_HEADS
CHUNK = 128
POOL_WINDOWS = (2, 4, 8, 16)
POOL_GROUPS = len(POOL_WINDOWS)
POOL_GROUP_DIM = POOL_WIDTH // POOL_GROUPS
IN_PROJ_WIDTH = 2 * GMLP_WIDTH + POOL_WIDTH
N_EXPERTS = 16
CAPACITY_FACTOR = 2
EXPERT_D_FF = D_MODEL
EPS = 1e-6

kernel_name = "hybrid_gmlp_pool_ec_encoder"


def rms_norm(x, g):
    xf = x.astype(jnp.float32)
    y = xf * lax.rsqrt(jnp.mean(xf * xf, axis=-1, keepdims=True) + EPS)
    return (y * g.astype(jnp.float32)).astype(x.dtype)


def gmlp_chunk_mixer(z, v_norm_g, w_spatial, b_spatial):
    B, S, _ = z.shape
    z = jax.nn.gelu(z)
    u, v = jnp.split(z, 2, axis=-1)
    v = rms_norm(v, v_norm_g)
    v = v.reshape(B, S // CHUNK, CHUNK, GMLP_HEADS, GMLP_HEAD_DIM)
    sv = jnp.einsum('hij,bcjhd->bcihd', w_spatial, v) + b_spatial.T[:, :, None]
    return u * sv.reshape(B, S, GMLP_WIDTH)


def pool_mixer(p, w_pool, b_pool, pool_scale):
    B, S, _ = p.shape
    pf = p.astype(jnp.float32).reshape(B, S, POOL_GROUPS, POOL_GROUP_DIM)
    cs = jnp.concatenate([jnp.zeros((B, 1, POOL_GROUPS, POOL_GROUP_DIM), jnp.float32),
                          jnp.cumsum(pf, axis=1)], axis=1)
    t = jnp.arange(S)
    outs = []
    for g, w in enumerate(POOL_WINDOWS):
        lo = jnp.maximum(t - w // 2, 0)
        hi = jnp.minimum(t + w // 2, S)
        cnt = (hi - lo).astype(jnp.float32)
        mean = (cs[:, hi, g] - cs[:, lo, g]) / cnt[None, :, None]
        outs.append(mean - pf[:, :, g])
    d = jnp.stack(outs, axis=2).astype(p.dtype)
    y = jnp.einsum('bsgi,gio->bsgo', d, w_pool) + b_pool
    return y.reshape(B, S, POOL_WIDTH) * pool_scale


def expert_choice_ffn(xn, w_router, w_gate, w_up, w_down):
    B, S, D = xn.shape
    T = B * S
    cap = CAPACITY_FACTOR * T // N_EXPERTS
    xt = xn.reshape(T, D)
    logits = jnp.einsum('td,de->te', xt, w_router).astype(jnp.float32)
    aff = jax.nn.softmax(logits, axis=-1)
    gate, idx = lax.top_k(aff.T, cap)
    xe = xt[idx]
    h = jax.nn.silu(jnp.einsum('ecd,edf->ecf', xe, w_gate)) * jnp.einsum('ecd,edf->ecf', xe, w_up)
    ye = jnp.einsum('ecf,efd->ecd', h, w_down) * gate[..., None].astype(xn.dtype)
    y = jnp.zeros((T, D), xn.dtype).at[idx.reshape(-1)].add(ye.reshape(-1, D))
    return y.reshape(B, S, D)


def encoder_layer(x, norm1_g, w_in, v_norm_g, w_spatial, b_spatial, w_pool, b_pool,
                  pool_scale, w_out, norm2_g, w_router, w_gate, w_up, w_down):
    h = rms_norm(x, norm1_g)
    z = jnp.einsum('bsd,dk->bsk', h, w_in)
    ya = gmlp_chunk_mixer(z[..., :2 * GMLP_WIDTH], v_norm_g, w_spatial, b_spatial)
    yb = pool_mixer(z[..., 2 * GMLP_WIDTH:], w_pool, b_pool, pool_scale)
    x = x + jnp.einsum('bsk,kd->bsd', jnp.concatenate([ya, yb], axis=-1), w_out)
    x = x + expert_choice_ffn(rms_norm(x, norm2_g), w_router, w_gate, w_up, w_down)
    return x


def setup_inputs(seed: int = 0) -> dict:
    key = jax.random.key(seed)
    ks = jax.random.split(key, 20)
    f32 = jnp.float32
    nrm = lambda k, shape, s: jax.random.normal(k, shape, f32) * s
    L = DEPTH
    return {
        "x_prompt": jax.random.normal(ks[0], (BATCH, SEQ, D_MODEL), f32),
        "x_sample": jax.random.normal(ks[1], (DEC_BATCH, DEC_SEQ, D_MODEL), f32),
        "norm1_g": 1.0 + nrm(ks[2], (L, D_MODEL), 0.02),
        "w_in": nrm(ks[3], (L, D_MODEL, IN_PROJ_WIDTH), D_MODEL ** -0.5),
        "v_norm_g": 1.0 + nrm(ks[4], (L, GMLP_WIDTH), 0.02),
        "w_spatial": nrm(ks[5], (L, GMLP_HEADS, CHUNK, CHUNK), CHUNK ** -0.5),
        "b_spatial": 1.0 + nrm(ks[6], (L, GMLP_HEADS, CHUNK), 0.02),
        "w_pool": nrm(ks[7], (L, POOL_GROUPS, POOL_GROUP_DIM, POOL_GROUP_DIM), POOL_GROUP_DIM ** -0.5),
        "b_pool": nrm(ks[8], (L, POOL_GROUPS, POOL_GROUP_DIM), 0.02),
        "pool_scale": jax.random.uniform(ks[9], (L, POOL_WIDTH), f32, 0.5, 1.0),
        "w_out": nrm(ks[10], (L, MIX_WIDTH, D_MODEL), MIX_WIDTH ** -0.5),
        "norm2_g": 1.0 + nrm(ks[11], (L, D_MODEL), 0.02),
        "w_router": nrm(ks[12], (L, D_MODEL, N_EXPERTS), D_MODEL ** -0.5),
        "w_gate": nrm(ks[13], (L, N_EXPERTS, D_MODEL, EXPERT_D_FF), D_MODEL ** -0.5),
        "w_up": nrm(ks[14], (L, N_EXPERTS, D_MODEL, EXPERT_D_FF), D_MODEL ** -0.5),
        "w_down": nrm(ks[15], (L, N_EXPERTS, EXPERT_D_FF, D_MODEL), EXPERT_D_FF ** -0.5),
        "final_norm_g": 1.0 + nrm(ks[16], (D_MODEL,), 0.02),
    }


def reference(x_prompt, x_sample, norm1_g, w_in, v_norm_g, w_spatial, b_spatial, w_pool,
              b_pool, pool_scale, w_out, norm2_g, w_router, w_gate, w_up, w_down, final_norm_g):
    def trunk(x):
        for l in range(DEPTH):
            x = encoder_layer(x, norm1_g[l], w_in[l], v_norm_g[l], w_spatial[l], b_spatial[l],
                              w_pool[l], b_pool[l], pool_scale[l], w_out[l], norm2_g[l],
                              w_router[l], w_gate[l], w_up[l], w_down[l])
        return rms_norm(x, final_norm_g)

    y_prompt = trunk(x_prompt)
    y_sample = trunk(x_sample)
    return (y_prompt, y_sample)
```

```python
import functools
import math

import jax
import jax.numpy as jnp
from jax import lax
from jax.experimental import pallas as pl
from jax.experimental.pallas import tpu as pltpu

EPS = 1e-6
CHUNK = 128
GMLP_HEADS = 4
POOL_WINDOWS = (2, 4, 8, 16)
HALO = 8
N_EXPERTS = 16
CAPACITY_FACTOR = 2

LANES = 128
MIXER_TM = 256
EXPERT_TM = 1024
EXPERT_TF = 256
ROUTE_W = 512
VMEM_LIMIT = 56 * 1024 * 1024


def _rms(x, g):
    return x * lax.rsqrt(jnp.mean(x * x, axis=-1, keepdims=True) + EPS) * g


def _gelu_tanh(x):
    c = math.sqrt(2.0 / math.pi)
    return 0.5 * x * (1.0 + jnp.tanh(c * (x + 0.044715 * (x * x * x))))


def _dot(a, b):
    return jnp.dot(a, b, preferred_element_type=jnp.float32)


def _mixer_kernel(x_ref, xp_ref, xn_ref, g1_ref, win_ref, vg_ref, ws_ref, bs_ref,
                  wp_ref, bp_ref, ps_ref, wout_ref, g2_ref, wr_ref,
                  x1_ref, xnorm_ref, aff_ref, mix_ref, *, seq_len):
    tm = x_ref.shape[0]
    gw = vg_ref.shape[1]
    hd = gw // GMLP_HEADS
    pw = ps_ref.shape[1]
    gd = pw // len(POOL_WINDOWS)
    s0 = (pl.program_id(0) * tm) % seq_len

    x = x_ref[...]
    g1 = g1_ref[...]
    h = _rms(x, g1)
    h_ext = jnp.concatenate([h, _rms(xp_ref[...], g1), _rms(xn_ref[...], g1)], axis=0)
    h_ext = h_ext.astype(jnp.bfloat16)

    z_uv = _gelu_tanh(_dot(h_ext[:tm], win_ref[:, :2 * gw]))
    z_p = _dot(h_ext, win_ref[:, 2 * gw:])

    u = z_uv[:, :gw]
    v = _rms(z_uv[:, gw:], vg_ref[...]).astype(jnp.bfloat16)
    bs = bs_ref[...]
    for c in range(tm // CHUNK):
        rows = slice(c * CHUNK, (c + 1) * CHUNK)
        for hh in range(GMLP_HEADS):
            cols = slice(hh * hd, (hh + 1) * hd)
            sv = _dot(ws_ref[hh], v[rows, cols]) + bs[:, hh:hh + 1]
            mix_ref[rows, cols] = (u[rows, cols] * sv).astype(jnp.bfloat16)

    p = z_p[:tm]
    pe = z_p.astype(jnp.bfloat16)
    ri = lax.broadcasted_iota(jnp.int32, (tm, tm + 2 * HALO), 0) + s0
    q = lax.broadcasted_iota(jnp.int32, (tm, tm + 2 * HALO), 1)
    qpos = jnp.where(q < tm, q, jnp.where(q < tm + HALO, q - tm - HALO, q - HALO)) + s0
    rpos = lax.broadcasted_iota(jnp.int32, (tm, 1), 0) + s0
    for g, w in enumerate(POOL_WINDOWS):
        cols = slice(g * gd, (g + 1) * gd)
        lo = jnp.maximum(ri - w // 2, 0)
        hi = jnp.minimum(ri + w // 2, seq_len)
        band = jnp.where((qpos >= lo) & (qpos < hi), 1.0, 0.0).astype(jnp.bfloat16)
        cnt = jnp.minimum(rpos + w // 2, seq_len) - jnp.maximum(rpos - w // 2, 0)
        d = _dot(band, pe[:, cols]) / cnt.astype(jnp.float32) - p[:, cols]
        yb = _dot(d.astype(jnp.bfloat16), wp_ref[g]) + bp_ref[g:g + 1, :]
        mix_ref[:, gw + g * gd:gw + (g + 1) * gd] = (yb * ps_ref[:, cols]).astype(jnp.bfloat16)

    x1 = x + _dot(mix_ref[...], wout_ref[...])
    x1_ref[...] = x1
    xnorm = _rms(x1, g2_ref[...])
    xnorm_ref[...] = xnorm
    logits = lax.dot_general(wr_ref[...], xnorm.astype(jnp.bfloat16),
                             (((1,), (1,)), ((), ())),
                             preferred_element_type=jnp.float32)
    e = jnp.exp(logits - jnp.max(logits, axis=0, keepdims=True))
    aff_ref[...] = e / jnp.sum(e, axis=0, keepdims=True)


def _mixer(x2d, seq_len, g1, win, vg, ws, bs_t, wp, bp, ps, wout, g2, wr_t):
    t, d = x2d.shape
    tm = MIXER_TM
    assert seq_len % tm == 0 and tm % CHUNK == 0 and t % seq_len == 0
    hb = tm // HALO
    last = t // HALO - 1
    const = lambda shape: pl.BlockSpec(shape, lambda i: (0,) * len(shape),
                                       pipeline_mode=pl.Buffered(1))
    return pl.pallas_call(
        functools.partial(_mixer_kernel, seq_len=seq_len),
        grid=(t // tm,),
        in_specs=[
            pl.BlockSpec((tm, d), lambda i: (i, 0)),
            pl.BlockSpec((HALO, d), lambda i: (jnp.maximum(i * hb - 1, 0), 0)),
            pl.BlockSpec((HALO, d), lambda i: (jnp.minimum((i + 1) * hb, last), 0)),
            const(g1.shape), const(win.shape), const(vg.shape), const(ws.shape),
            const(bs_t.shape), const(wp.shape), const(bp.shape), const(ps.shape),
            const(wout.shape), const(g2.shape), const(wr_t.shape),
        ],
        out_specs=[
            pl.BlockSpec((tm, d), lambda i: (i, 0)),
            pl.BlockSpec((tm, d), lambda i: (i, 0)),
            pl.BlockSpec((N_EXPERTS, tm), lambda i: (0, i)),
        ],
        out_shape=[
            jax.ShapeDtypeStruct((t, d), jnp.float32),
            jax.ShapeDtypeStruct((t, d), jnp.float32),
            jax.ShapeDtypeStruct((N_EXPERTS, t), jnp.float32),
        ],
        scratch_shapes=[pltpu.VMEM((tm, wout.shape[0]), jnp.bfloat16)],
        compiler_params=pltpu.CompilerParams(
            dimension_semantics=("arbitrary",), vmem_limit_bytes=VMEM_LIMIT),
        name="mixer",
    )(x2d, x2d, x2d, g1, win, vg, ws, bs_t, wp, bp, ps, wout, g2, wr_t)


def _route_kernel(a_ref, idx_ref, gate_ref, *, cap):
    a = a_ref[0]
    nb = a.shape[0]
    bits = pltpu.bitcast(a, jnp.int32)

    def count(mask):
        s = jnp.sum(jnp.where(mask, 1.0, 0.0), axis=0, keepdims=True)
        return jnp.sum(s, axis=1, keepdims=True)

    def bisect(i, cur):
        cand = cur | jnp.left_shift(jnp.int32(1), 30 - i)
        return jnp.where(count(bits >= cand) >= cap, cand, cur)
    thr = lax.fori_loop(0, 31, bisect, jnp.zeros((1, 1), jnp.int32))

    jj = lax.broadcasted_iota(jnp.int32, (LANES, LANES), 0)
    kk = lax.broadcasted_iota(jnp.int32, (LANES, LANES), 1)
    upper = jnp.where(jj <= kk, 1.0, 0.0).astype(jnp.bfloat16)
    bi = lax.broadcasted_iota(jnp.int32, (nb, nb), 0)
    bk = lax.broadcasted_iota(jnp.int32, (nb, nb), 1)
    lower = jnp.where(bk <= bi, 1.0, 0.0).astype(jnp.bfloat16)

    def ranks(mask):
        m = jnp.where(mask, 1.0, 0.0).astype(jnp.bfloat16)
        local = _dot(m, upper)
        p_in = jnp.sum(_dot(lower, m), axis=1, keepdims=True)
        p_ex = p_in - local[:, LANES - 1:LANES]
        return local, p_in, p_ex

    gt = bits > thr
    eq = bits == thr
    need = cap - count(gt)
    eq_local, _, eq_pex = ranks(eq)
    sel = gt | (eq & (eq_pex + eq_local <= need))
    local, p_in, p_ex = ranks(sel)

    a_hi = a.astype(jnp.bfloat16)
    r1 = a - a_hi.astype(jnp.float32)
    a_mid = r1.astype(jnp.bfloat16)
    a_lo = (r1 - a_mid.astype(jnp.float32)).astype(jnp.bfloat16)
    local_t = local.T.astype(jnp.bfloat16)
    hi_t = a_hi.astype(jnp.float32).T.astype(jnp.bfloat16)
    mid_t = a_mid.astype(jnp.float32).T.astype(jnp.bfloat16)
    lo_t = a_lo.astype(jnp.float32).T.astype(jnp.bfloat16)
    jcol = lax.broadcasted_iota(jnp.int32, (LANES, 1), 0).astype(jnp.float32)

    w = idx_ref.shape[2]
    def resolve(ci, carry):
        c = (lax.broadcasted_iota(jnp.int32, (1, w), 1) + ci * w).astype(jnp.float32)
        in_blk = (p_ex <= c) & (c < p_in)
        onehot = jnp.where(in_blk, 1.0, 0.0).astype(jnp.bfloat16)
        blk = jnp.sum(jnp.where(p_in <= c, 1.0, 0.0), axis=0, keepdims=True)
        lc = c - jnp.sum(jnp.where(in_blk, p_ex, 0.0), axis=0, keepdims=True)
        g_local = _dot(local_t, onehot)
        jstar = jnp.sum(jnp.where(g_local <= lc, 1.0, 0.0), axis=0, keepdims=True)
        idx_ref[0, pl.ds(ci, 1), :] = (blk * LANES + jstar).astype(jnp.int32)
        g_aff = (_dot(hi_t, onehot) + _dot(mid_t, onehot)) + _dot(lo_t, onehot)
        gate_ref[0, pl.ds(ci, 1), :] = jnp.sum(
            jnp.where(jcol == jstar, g_aff, 0.0), axis=0, keepdims=True)
        return carry
    lax.fori_loop(0, cap // w, resolve, 0)


def _route(aff_t):
    ne, t = aff_t.shape
    cap = CAPACITY_FACTOR * t // ne
    nb = t // LANES
    w = ROUTE_W
    assert t % LANES == 0 and nb % 8 == 0 and cap % w == 0
    idx, gate = pl.pallas_call(
        functools.partial(_route_kernel, cap=cap),
        grid=(ne,),
        in_specs=[pl.BlockSpec((1, nb, LANES), lambda e: (e, 0, 0))],
        out_specs=[pl.BlockSpec((1, cap // w, w), lambda e: (e, 0, 0)),
                   pl.BlockSpec((1, cap // w, w), lambda e: (e, 0, 0))],
        out_shape=[jax.ShapeDtypeStruct((ne, cap // w, w), jnp.int32),
                   jax.ShapeDtypeStruct((ne, cap // w, w), jnp.float32)],
        compiler_params=pltpu.CompilerParams(dimension_semantics=("arbitrary",)),
        name="route",
    )(aff_t.reshape(ne, nb, LANES))
    return idx.reshape(ne, cap), gate.reshape(ne, cap)


def _row_copy(hbm, vmem, t, r, sem, to_hbm):
    src, dst = hbm.at[pl.ds(t, 1)], vmem.at[pl.ds(r, 1)]
    return pltpu.make_async_copy(dst, src, sem) if to_hbm else pltpu.make_async_copy(src, dst, sem)


def _expert_kernel(idx_ref, gate_ref, wg_ref, wu_ref, wd_ref,
                   xn_p, xn_s, yin_p, yin_s, y_p, y_s,
                   xbuf, xbf, ybuf, acc, sems, *, prompt_tiles):
    del yin_p, yin_s
    m = pl.program_id(1)
    f = pl.program_id(2)
    tm = xbuf.shape[0]

    def for_group(fn):
        @pl.when(m < prompt_tiles)
        def _():
            fn(xn_p, y_p)

        @pl.when(m >= prompt_tiles)
        def _():
            fn(xn_s, y_s)

    @pl.when(f == 0)
    def _():
        def gather(xn_hbm, y_hbm):
            def issue(r, c):
                t = idx_ref[0, 0, r]
                _row_copy(xn_hbm, xbuf, t, r, sems.at[0], False).start()
                _row_copy(y_hbm, ybuf, t, r, sems.at[1], False).start()
                return c
            lax.fori_loop(0, tm, issue, 0, unroll=8)
            pltpu.make_async_copy(xn_hbm.at[pl.ds(0, tm)], xbuf, sems.at[0]).wait()
            pltpu.make_async_copy(y_hbm.at[pl.ds(0, tm)], ybuf, sems.at[1]).wait()
        for_group(gather)
        xbf[...] = xbuf[...].astype(jnp.bfloat16)
        acc[...] = jnp.zeros_like(acc)

    xb = xbf[...]
    g = _dot(xb, wg_ref[...].astype(jnp.bfloat16))
    u = _dot(xb, wu_ref[...].astype(jnp.bfloat16))
    hcol = (g * (1.0 / (1.0 + jnp.exp(-g))) * u).astype(jnp.bfloat16)
    acc[...] += _dot(hcol, wd_ref[...].astype(jnp.bfloat16))

    @pl.when(f == pl.num_programs(2) - 1)
    def _():
        gate_col = jnp.broadcast_to(gate_ref[0], (LANES, tm)).T
        for j in range(acc.shape[1] // LANES):
            cols = slice(j * LANES, (j + 1) * LANES)
            ybuf[:, cols] = ybuf[:, cols] + acc[:, cols] * gate_col

        def scatter(xn_hbm, y_hbm):
            del xn_hbm
            def issue(r, c):
                t = idx_ref[0, 0, r]
                _row_copy(y_hbm, ybuf, t, r, sems.at[1], True).start()
                return c
            lax.fori_loop(0, tm, issue, 0, unroll=8)
            pltpu.make_async_copy(ybuf, y_hbm.at[pl.ds(0, tm)], sems.at[1]).wait()
        for_group(scatter)


def _experts(idx_all, gate_all, wg, wu, wd, xn_p, xn_s, y_p, y_s, prompt_tiles):
    ne, d, dff = wg.shape
    tm, tf = EXPERT_TM, EXPERT_TF
    tiles = idx_all.shape[1] // tm
    idx3 = idx_all.reshape(ne * tiles, 1, tm)
    gate3 = gate_all.reshape(ne * tiles, 1, tm)
    any_spec = pl.BlockSpec(memory_space=pl.ANY)
    return pl.pallas_call(
        functools.partial(_expert_kernel, prompt_tiles=prompt_tiles),
        grid=(ne, tiles, dff // tf),
        in_specs=[
            pl.BlockSpec((1, 1, tm), lambda e, m, f: (e * tiles + m, 0, 0),
                         memory_space=pltpu.SMEM),
            pl.BlockSpec((1, 1, tm), lambda e, m, f: (e * tiles + m, 0, 0)),
            pl.BlockSpec((None, d, tf), lambda e, m, f: (e, 0, f)),
            pl.BlockSpec((None, d, tf), lambda e, m, f: (e, 0, f)),
            pl.BlockSpec((None, tf, d), lambda e, m, f: (e, f, 0)),
            any_spec, any_spec, any_spec, any_spec,
        ],
        out_specs=[any_spec, any_spec],
        out_shape=[jax.ShapeDtypeStruct(y_p.shape, y_p.dtype),
                   jax.ShapeDtypeStruct(y_s.shape, y_s.dtype)],
        input_output_aliases={7: 0, 8: 1},
        scratch_shapes=[
            pltpu.VMEM((tm, d), jnp.float32),
            pltpu.VMEM((tm, d), jnp.bfloat16),
            pltpu.VMEM((tm, d), jnp.float32),
            pltpu.VMEM((tm, d), jnp.float32),
            pltpu.SemaphoreType.DMA((2,)),
        ],
        compiler_params=pltpu.CompilerParams(
            dimension_semantics=("arbitrary", "arbitrary", "arbitrary"),
            vmem_limit_bytes=VMEM_LIMIT),
        name="experts",
    )(idx3, gate3, wg, wu, wd, xn_p, xn_s, y_p, y_s)


def _final_kernel(y_ref, g_ref, o_ref):
    o_ref[...] = _rms(y_ref[...], g_ref[...])


def _final_norm(y2d, g):
    t, d = y2d.shape
    tm = 512
    return pl.pallas_call(
        _final_kernel,
        grid=(t // tm,),
        in_specs=[pl.BlockSpec((tm, d), lambda i: (i, 0)),
                  pl.BlockSpec((1, d), lambda i: (0, 0))],
        out_specs=pl.BlockSpec((tm, d), lambda i: (i, 0)),
        out_shape=jax.ShapeDtypeStruct((t, d), jnp.float32),
        compiler_params=pltpu.CompilerParams(dimension_semantics=("arbitrary",)),
        name="final_norm",
    )(y2d, g)


def kernel(x_prompt, x_sample, norm1_g, w_in, v_norm_g, w_spatial, b_spatial, w_pool,
           b_pool, pool_scale, w_out, norm2_g, w_router, w_gate, w_up, w_down, final_norm_g):
    assert norm1_g.shape[0] == 1, "single-layer block"
    bf = jnp.bfloat16
    row = lambda a: a.reshape(1, -1)
    mixer_w = (row(norm1_g[0]), w_in[0].astype(bf), row(v_norm_g[0]), w_spatial[0].astype(bf),
               b_spatial[0].T, w_pool[0].astype(bf), b_pool[0], row(pool_scale[0]),
               w_out[0].astype(bf), row(norm2_g[0]), w_router[0].T.astype(bf))

    groups = []
    for x in (x_prompt, x_sample):
        b, s, d = x.shape
        x1, xn, aff_t = _mixer(x.reshape(b * s, d), s, *mixer_w)
        idx, gate = _route(aff_t)
        groups.append((x1, xn, idx, gate, x.shape))
    (y_p, xn_p, idx_p, gate_p, shp_p), (y_s, xn_s, idx_s, gate_s, shp_s) = groups

    assert idx_p.shape[1] % EXPERT_TM == 0 and idx_s.shape[1] % EXPERT_TM == 0
    idx_all = jnp.concatenate([idx_p, idx_s], axis=1)
    gate_all = jnp.concatenate([gate_p, gate_s], axis=1)
    y_p, y_s = _experts(idx_all, gate_all, w_gate[0], w_up[0], w_down[0],
                        xn_p, xn_s, y_p, y_s, idx_p.shape[1] // EXPERT_TM)

    fg = row(final_norm_g)
    return (_final_norm(y_p, fg).reshape(shp_p), _final_norm(y_s, fg).reshape(shp_s))
```

```python
import functools
import math

import jax
import jax.numpy as jnp
from jax import lax
from jax.experimental import pallas as pl
from jax.experimental.pallas import tpu as pltpu

EPS = 1e-6
CHUNK = 128
GMLP_HEADS = 4
POOL_WINDOWS = (2, 4, 8, 16)
HALO = 8
N_EXPERTS = 16
CAPACITY_FACTOR = 2

LANES = 128
BF16_ROWS = 16
MIXER_TM = 256
EXPERT_TM = 1024
EXPERT_TF = 256
ROUTE_W = 512
VMEM_LIMIT = 56 * 1024 * 1024


def _rms(x, g):
    return x * lax.rsqrt(jnp.mean(x * x, axis=-1, keepdims=True) + EPS) * g


def _gelu_tanh(x):
    c = math.sqrt(2.0 / math.pi)
    return 0.5 * x * (1.0 + jnp.tanh(c * (x + 0.044715 * (x * x * x))))


def _dot(a, b):
    return jnp.dot(a, b, preferred_element_type=jnp.float32)


def _to_token_major(ref, x):
    n, rows = x.shape[0], x.shape[1] // LANES
    for k in range(rows):
        ref[pl.ds(k, n, stride=rows), :] = x[:, k * LANES:(k + 1) * LANES]


def _from_token_major(ref, n):
    rows = ref.shape[0] // n
    for k in range(rows):
        yield ref[pl.ds(k, n, stride=rows), :]


def _mixer_kernel(xa_ref, xa_prev_ref, xa_next_ref, xb_ref, xb_prev_ref, xb_next_ref,
                  g1_ref, win_ref, vg_ref, ws_ref, bs_ref, wp_ref, bp_ref, ps_ref, wout_ref,
                  g2_ref, wr_ref, x1_ref, xnorm_ref, aff_ref, mix_ref, *, tiles_a, tiles_b,
                  seq_a, seq_b):
    tm = xa_ref.shape[0]
    gw = vg_ref.shape[1]
    hd = gw // GMLP_HEADS
    pw = ps_ref.shape[1]
    gd = pw // len(POOL_WINDOWS)
    i = pl.program_id(0)
    in_a = i < tiles_a
    seq_len = jnp.where(in_a, seq_a, seq_b)
    tile = jnp.where(in_a, i, jnp.minimum(i - tiles_a, tiles_b - 1))
    s0 = lax.rem(tile * tm, seq_len)

    x = jnp.where(in_a, xa_ref[...], xb_ref[...])
    g1 = g1_ref[...]
    h = _rms(x, g1)
    h_ext = jnp.concatenate([h, _rms(jnp.where(in_a, xa_prev_ref[...], xb_prev_ref[...]), g1),
                             _rms(jnp.where(in_a, xa_next_ref[...], xb_next_ref[...]), g1)], axis=0)
    h_ext = h_ext.astype(jnp.bfloat16)

    z_uv = _gelu_tanh(_dot(h_ext[:tm], win_ref[:, :2 * gw]))
    z_p = _dot(h_ext, win_ref[:, 2 * gw:])

    u = z_uv[:, :gw]
    v = _rms(z_uv[:, gw:], vg_ref[...]).astype(jnp.bfloat16)
    bs = bs_ref[...]
    for c in range(tm // CHUNK):
        rows = slice(c * CHUNK, (c + 1) * CHUNK)
        for hh in range(GMLP_HEADS):
            cols = slice(hh * hd, (hh + 1) * hd)
            sv = _dot(ws_ref[hh], v[rows, cols]) + bs[:, hh:hh + 1]
            mix_ref[rows, cols] = (u[rows, cols] * sv).astype(jnp.bfloat16)

    p = z_p[:tm]
    pe = z_p.astype(jnp.bfloat16)
    ri = lax.broadcasted_iota(jnp.int32, (tm, tm + 2 * HALO), 0) + s0
    q = lax.broadcasted_iota(jnp.int32, (tm, tm + 2 * HALO), 1)
    qpos = jnp.where(q < tm, q, jnp.where(q < tm + HALO, q - tm - HALO, q - HALO)) + s0
    rpos = lax.broadcasted_iota(jnp.int32, (tm, 1), 0) + s0
    for g, w in enumerate(POOL_WINDOWS):
        cols = slice(g * gd, (g + 1) * gd)
        lo = jnp.maximum(ri - w // 2, 0)
        hi = jnp.minimum(ri + w // 2, seq_len)
        band = jnp.where((qpos >= lo) & (qpos < hi), 1.0, 0.0).astype(jnp.bfloat16)
        cnt = jnp.minimum(rpos + w // 2, seq_len) - jnp.maximum(rpos - w // 2, 0)
        d = _dot(band, pe[:, cols]) / cnt.astype(jnp.float32) - p[:, cols]
        yb = _dot(d.astype(jnp.bfloat16), wp_ref[g]) + bp_ref[g:g + 1, :]
        mix_ref[:, gw + g * gd:gw + (g + 1) * gd] = (yb * ps_ref[:, cols]).astype(jnp.bfloat16)

    x1 = x + _dot(mix_ref[...], wout_ref[...])
    x1_ref[...] = x1
    xb = _rms(x1, g2_ref[...]).astype(jnp.bfloat16)
    xw = pltpu.bitcast(xb.astype(jnp.float32), jnp.uint32)
    words = [(xw[:, (2 * c + 1) * LANES:(2 * c + 2) * LANES] & jnp.uint32(0xFFFF0000))
             | (xw[:, 2 * c * LANES:(2 * c + 1) * LANES] >> 16) for c in range(xw.shape[1] // (2 * LANES))]
    _to_token_major(xnorm_ref, jnp.concatenate(words, axis=1))
    logits = lax.dot_general(wr_ref[...], xb, (((1,), (1,)), ((), ())),
                             preferred_element_type=jnp.float32)
    e = jnp.exp(logits - jnp.max(logits, axis=0, keepdims=True))
    aff_ref[...] = e / jnp.sum(e, axis=0, keepdims=True)


def _mixer(xa, xb, weights, *, pad_rows):
    d = xa.shape[-1]
    tm = MIXER_TM
    seq_a, seq_b = xa.shape[1], xb.shape[1]
    ta, tb = xa.shape[0] * seq_a, xb.shape[0] * seq_b
    assert seq_a % tm == 0 and seq_b % tm == 0 and tm % CHUNK == 0 and pad_rows % tm == 0
    tiles_a, tiles_b = ta // tm, tb // tm
    n_tiles = tiles_a + tiles_b
    hb = tm // HALO
    xr = d // (2 * LANES)

    def group_specs(first, tiles):
        tile = lambda i: jnp.clip(i - first, 0, tiles - 1)
        last = tiles * hb - 1
        return [pl.BlockSpec((tm, d), lambda i: (tile(i), 0)),
                pl.BlockSpec((HALO, d), lambda i: (jnp.maximum(tile(i) * hb - 1, 0), 0)),
                pl.BlockSpec((HALO, d), lambda i: (jnp.minimum((tile(i) + 1) * hb, last), 0))]
    const = lambda a: pl.BlockSpec(a.shape, lambda i: (0,) * a.ndim, pipeline_mode=pl.Buffered(1))
    real = lambda i: jnp.minimum(i, n_tiles - 1)
    xa2, xb2 = xa.reshape(ta, d), xb.reshape(tb, d)
    return pl.pallas_call(
        functools.partial(_mixer_kernel, tiles_a=tiles_a, tiles_b=tiles_b, seq_a=seq_a, seq_b=seq_b),
        grid=(n_tiles + pad_rows // tm,),
        in_specs=[*group_specs(0, tiles_a), *group_specs(tiles_a, tiles_b),
                  *[const(w) for w in weights]],
        out_specs=[
            pl.BlockSpec((tm, d), lambda i: (i, 0)),
            pl.BlockSpec((tm * xr, LANES), lambda i: (real(i), 0)),
            pl.BlockSpec((N_EXPERTS, tm), lambda i: (0, real(i))),
        ],
        out_shape=[
            jax.ShapeDtypeStruct((ta + tb + pad_rows, d), jnp.float32),
            jax.ShapeDtypeStruct(((ta + tb) * xr, LANES), jnp.uint32),
            jax.ShapeDtypeStruct((N_EXPERTS, ta + tb), jnp.float32),
        ],
        scratch_shapes=[pltpu.VMEM((tm, weights[8].shape[0]), jnp.bfloat16)],
        compiler_params=pltpu.CompilerParams(
            dimension_semantics=("arbitrary",), vmem_limit_bytes=VMEM_LIMIT),
        name="mixer",
    )(xa2, xa2, xa2, xb2, xb2, xb2, *weights)


def _threshold_kernel(a_ref, thr_ref, *, cap):
    ne = a_ref.shape[0]

    def bisect(i, cur):
        cand = cur | jnp.left_shift(jnp.int32(1), 30 - i)
        cand_f = lax.bitcast_convert_type(cand, jnp.float32)
        n = jnp.sum(jnp.where(a_ref[...] >= cand_f, 1.0, 0.0), axis=1, keepdims=True)
        return jnp.where(n >= cap, cand, cur)
    thr = lax.fori_loop(0, 31, bisect, jnp.zeros((ne, 1), jnp.int32))
    thr_ref[...] = jnp.broadcast_to(lax.bitcast_convert_type(thr, jnp.float32), thr_ref.shape)


def _route_kernel(a_ref, thr_ref, idx_ref, gate_ref, *, cap, token_offset):
    a = a_ref[0]
    nb = a.shape[0]
    thr = thr_ref[0]

    def count(mask):
        s = jnp.sum(jnp.where(mask, 1.0, 0.0), axis=0, keepdims=True)
        return jnp.sum(s, axis=1, keepdims=True)

    jj = lax.broadcasted_iota(jnp.int32, (LANES, LANES), 0)
    kk = lax.broadcasted_iota(jnp.int32, (LANES, LANES), 1)
    upper = jnp.where(jj <= kk, 1.0, 0.0).astype(jnp.bfloat16)
    bi = lax.broadcasted_iota(jnp.int32, (nb, nb), 0)
    bk = lax.broadcasted_iota(jnp.int32, (nb, nb), 1)
    lower = jnp.where(bk <= bi, 1.0, 0.0).astype(jnp.bfloat16)

    def ranks(mask):
        m = jnp.where(mask, 1.0, 0.0).astype(jnp.bfloat16)
        local = _dot(m, upper)
        p_in = jnp.sum(_dot(lower, m), axis=1, keepdims=True)
        p_ex = p_in - local[:, LANES - 1:LANES]
        return local, p_in, p_ex

    gt = a > thr
    eq = a == thr
    need = cap - count(gt)
    eq_local, _, eq_pex = ranks(eq)
    sel = gt | (eq & (eq_pex + eq_local <= need))
    local, p_in, p_ex = ranks(sel)

    a_hi = a.astype(jnp.bfloat16)
    r1 = a - a_hi.astype(jnp.float32)
    a_mid = r1.astype(jnp.bfloat16)
    a_lo = (r1 - a_mid.astype(jnp.float32)).astype(jnp.bfloat16)
    local_t = local.T.astype(jnp.bfloat16)
    hi_t = a_hi.astype(jnp.float32).T.astype(jnp.bfloat16)
    mid_t = a_mid.astype(jnp.float32).T.astype(jnp.bfloat16)
    lo_t = a_lo.astype(jnp.float32).T.astype(jnp.bfloat16)
    jcol = lax.broadcasted_iota(jnp.int32, (LANES, 1), 0).astype(jnp.float32)

    w = idx_ref.shape[2]
    def resolve(ci, carry):
        c = (lax.broadcasted_iota(jnp.int32, (1, w), 1) + ci * w).astype(jnp.float32)
        in_blk = (p_ex <= c) & (c < p_in)
        onehot = jnp.where(in_blk, 1.0, 0.0).astype(jnp.bfloat16)
        blk = jnp.sum(jnp.where(p_in <= c, 1.0, 0.0), axis=0, keepdims=True)
        lc = c - jnp.sum(jnp.where(in_blk, p_ex, 0.0), axis=0, keepdims=True)
        g_local = _dot(local_t, onehot)
        jstar = jnp.sum(jnp.where(g_local <= lc, 1.0, 0.0), axis=0, keepdims=True)
        idx_ref[0, pl.ds(ci, 1), :] = (blk * LANES + jstar).astype(jnp.int32) + token_offset
        g_aff = (_dot(hi_t, onehot) + _dot(mid_t, onehot)) + _dot(lo_t, onehot)
        gate_ref[0, pl.ds(ci, 1), :] = jnp.sum(
            jnp.where(jcol == jstar, g_aff, 0.0), axis=0, keepdims=True)
        return carry
    lax.fori_loop(0, cap // w, resolve, 0)


def _route(aff_t, token_offset):
    ne, t = aff_t.shape
    cap = CAPACITY_FACTOR * t // ne
    nb = t // LANES
    w = ROUTE_W
    assert t % LANES == 0 and nb % 8 == 0 and cap % w == 0
    thr = pl.pallas_call(
        functools.partial(_threshold_kernel, cap=cap),
        out_shape=jax.ShapeDtypeStruct((ne, LANES), jnp.float32),
        name="threshold",
    )(aff_t)
    idx, gate = pl.pallas_call(
        functools.partial(_route_kernel, cap=cap, token_offset=token_offset),
        grid=(ne,),
        in_specs=[pl.BlockSpec((1, nb, LANES), lambda e: (e, 0, 0)),
                  pl.BlockSpec((1, 1, LANES), lambda e: (e, 0, 0))],
        out_specs=[pl.BlockSpec((1, cap // w, w), lambda e: (e, 0, 0)),
                   pl.BlockSpec((1, cap // w, w), lambda e: (e, 0, 0))],
        out_shape=[jax.ShapeDtypeStruct((ne, cap // w, w), jnp.int32),
                   jax.ShapeDtypeStruct((ne, cap // w, w), jnp.float32)],
        compiler_params=pltpu.CompilerParams(dimension_semantics=("arbitrary",)),
        name="route",
    )(aff_t.reshape(ne, nb, LANES), thr.reshape(ne, 1, LANES))
    return idx.reshape(ne, cap), gate.reshape(ne, cap)


def _token_copy(hbm, vmem, tok, slot, rows, sem, to_hbm):
    src, dst = hbm.at[pl.ds(tok * rows, rows)], vmem.at[pl.ds(slot * rows, rows)]
    return pltpu.make_async_copy(dst, src, sem) if to_hbm else pltpu.make_async_copy(src, dst, sem)


def _start_rows(hbm, idx_ref, vmem, r0, n, rows, sem, *, to_hbm):
    for r in range(n):
        _token_copy(hbm, vmem, idx_ref[0, 0, r0 + r], r0 + r, rows, sem, to_hbm).start()


def _start_rows_loop(hbm, idx_ref, vmem, n, rows, sem, *, to_hbm):
    def issue(r, c):
        _token_copy(hbm, vmem, idx_ref[0, 0, r], r, rows, sem, to_hbm).start()
        return c
    lax.fori_loop(0, n, issue, 0, unroll=8)


def _wait_rows(hbm, vmem, sem, *, to_hbm):
    rows = hbm.at[pl.ds(0, vmem.shape[0])]
    (pltpu.make_async_copy(vmem, rows, sem) if to_hbm
     else pltpu.make_async_copy(rows, vmem, sem)).wait()


def _expert_kernel(idx_prev, idx_cur, idx_next, gate_prev, gate_cur, wg_ref, wu_ref, wd_ref,
                   xn_hbm, yin_hbm, y_hbm, xbuf, xbf, pool, gcol, sem_x, sem_y, sem_s):
    del yin_hbm
    ybuf, sbuf, acc = pool.at[0], pool.at[1], pool.at[2]
    i = pl.program_id(0)
    f = pl.program_id(1)
    n_tiles = pl.num_programs(0)
    n_steps = pl.num_programs(1)
    tm = acc.shape[0]
    xr = xbuf.shape[1] // tm
    rows_per_step = tm // 8
    slot = i % 2
    pslot = 1 - slot

    def gated_update(gate_ref):
        gcol[...] = jnp.broadcast_to(gate_ref[0], (LANES, tm)).T
        rb = 64
        def rows_pass(b, carry):
            rows = pl.ds(pl.multiple_of(b * rb, rb), rb)
            gate_col = gcol[rows, :]
            for k in range(acc.shape[1] // LANES):
                cols = slice(k * LANES, (k + 1) * LANES)
                sbuf[rows, cols] = ybuf[rows, cols] + acc[rows, cols] * gate_col
            return carry
        lax.fori_loop(0, tm // rb, rows_pass, 0)

    @pl.when((i == 0) & (f == 0))
    def _():
        sbuf[...] = jnp.zeros(sbuf.shape, sbuf.dtype)
        _start_rows_loop(xn_hbm, idx_cur, xbuf.at[0], tm, xr, sem_x.at[0], to_hbm=False)

    @pl.when(f == 0)
    def _():
        @pl.when(i > 0)
        def _():
            _wait_rows(y_hbm, ybuf, sem_y, to_hbm=False)
            _wait_rows(y_hbm, sbuf, sem_s, to_hbm=True)
            gated_update(gate_prev)

        _wait_rows(xn_hbm, xbuf.at[slot], sem_x.at[slot], to_hbm=False)
        for c, w in enumerate(_from_token_major(xbuf.at[slot], tm)):
            lo = pltpu.bitcast(w << 16, jnp.float32)
            hi = pltpu.bitcast(w & jnp.uint32(0xFFFF0000), jnp.float32)
            xbf[:, 2 * c * LANES:(2 * c + 1) * LANES] = lo.astype(jnp.bfloat16)
            xbf[:, (2 * c + 1) * LANES:(2 * c + 2) * LANES] = hi.astype(jnp.bfloat16)

    r0 = f * rows_per_step
    _start_rows(xn_hbm, idx_next, xbuf.at[pslot], r0, rows_per_step, xr, sem_x.at[pslot], to_hbm=False)
    _start_rows(y_hbm, idx_cur, ybuf, r0, rows_per_step, 1, sem_y, to_hbm=False)
    _start_rows(y_hbm, idx_prev, sbuf, r0, rows_per_step, 1, sem_s, to_hbm=True)

    xb = xbf[...]
    g = _dot(xb, wg_ref[...].astype(jnp.bfloat16))
    u = _dot(xb, wu_ref[...].astype(jnp.bfloat16))
    hcol = (g * (1.0 / (1.0 + jnp.exp(-g))) * u).astype(jnp.bfloat16)
    contrib = _dot(hcol, wd_ref[...].astype(jnp.bfloat16))
    acc_dyn = pool.at[2 + jnp.right_shift(idx_cur[0, 0, 0], 30)]
    acc_dyn[...] = contrib + jnp.where(f > 0, acc_dyn[...], 0.0)

    @pl.when((i == n_tiles - 1) & (f == n_steps - 1))
    def _():
        _wait_rows(y_hbm, ybuf, sem_y, to_hbm=False)
        _wait_rows(y_hbm, sbuf, sem_s, to_hbm=True)
        _wait_rows(xn_hbm, xbuf.at[pslot], sem_x.at[pslot], to_hbm=False)
        gated_update(gate_cur)
        _start_rows_loop(y_hbm, idx_cur, sbuf, tm, 1, sem_s, to_hbm=True)
        _wait_rows(y_hbm, sbuf, sem_s, to_hbm=True)


def _experts(idx_tiles, gate_tiles, wg, wu, wd, xn_all, y_all):
    ne, d, dff = wg.shape
    tm, tf = EXPERT_TM, EXPERT_TF
    n_tiles = idx_tiles.shape[0]
    per_expert = n_tiles // ne
    n_steps = dff // tf
    xr = d // (2 * LANES)
    n_tok = xn_all.shape[0] // xr
    assert n_steps == 8 and tm % n_steps == 0 and y_all.shape[0] >= n_tok + tm
    rows = jnp.arange(tm, dtype=jnp.int32)[None]
    idx_ext = jnp.concatenate([rows + n_tok, idx_tiles, rows], axis=0)
    gate_ext = jnp.concatenate([jnp.zeros((1, tm), jnp.float32), gate_tiles], axis=0)
    idx_ext = idx_ext.reshape(n_tiles + 2, 1, tm)
    gate_ext = gate_ext.reshape(n_tiles + 1, 1, tm)
    any_spec = pl.BlockSpec(memory_space=pl.ANY)
    idx_spec = lambda k: pl.BlockSpec((1, 1, tm), lambda i, f: (i + k, 0, 0), memory_space=pltpu.SMEM)
    gate_spec = lambda k: pl.BlockSpec((1, 1, tm), lambda i, f: (i + k, 0, 0))
    return pl.pallas_call(
        _expert_kernel,
        grid=(n_tiles, n_steps),
        in_specs=[
            idx_spec(0), idx_spec(1), idx_spec(2), gate_spec(0), gate_spec(1),
            pl.BlockSpec((None, d, tf), lambda i, f: (i // per_expert, 0, f)),
            pl.BlockSpec((None, d, tf), lambda i, f: (i // per_expert, 0, f)),
            pl.BlockSpec((None, tf, d), lambda i, f: (i // per_expert, f, 0)),
            any_spec, any_spec,
        ],
        out_specs=any_spec,
        out_shape=jax.ShapeDtypeStruct(y_all.shape, y_all.dtype),
        input_output_aliases={9: 0},
        scratch_shapes=[
            pltpu.VMEM((2, tm * xr, LANES), xn_all.dtype),
            pltpu.VMEM((tm, d), jnp.bfloat16),
            pltpu.VMEM((3, tm, d), jnp.float32),
            pltpu.VMEM((tm, LANES), jnp.float32),
            pltpu.SemaphoreType.DMA((2,)), pltpu.SemaphoreType.DMA, pltpu.SemaphoreType.DMA,
        ],
        compiler_params=pltpu.CompilerParams(
            dimension_semantics=("arbitrary", "arbitrary"), vmem_limit_bytes=VMEM_LIMIT),
        name="experts",
    )(idx_ext, idx_ext, idx_ext, gate_ext, gate_ext, wg, wu, wd, xn_all, y_all)


def _final_kernel(y_ref, g_ref, o_ref):
    o_ref[...] = _rms(y_ref[...], g_ref[...])


def _final_norm(y_all, g, row_offset, t):
    d = g.shape[1]
    tm = 512
    ob = row_offset // tm
    assert row_offset % tm == 0 and t % tm == 0
    return pl.pallas_call(
        _final_kernel,
        grid=(t // tm,),
        in_specs=[pl.BlockSpec((tm, d), lambda i: (i + ob, 0)),
                  pl.BlockSpec((1, d), lambda i: (0, 0))],
        out_specs=pl.BlockSpec((tm, d), lambda i: (i, 0)),
        out_shape=jax.ShapeDtypeStruct((t, d), jnp.float32),
        compiler_params=pltpu.CompilerParams(dimension_semantics=("arbitrary",)),
        name="final_norm",
    )(y_all, g)


def kernel(x_prompt, x_sample, norm1_g, w_in, v_norm_g, w_spatial, b_spatial, w_pool,
           b_pool, pool_scale, w_out, norm2_g, w_router, w_gate, w_up, w_down, final_norm_g):
    assert norm1_g.shape[0] == 1, "single-layer block"
    bf = jnp.bfloat16
    row = lambda a: a.reshape(1, -1)
    mixer_w = (row(norm1_g[0]), w_in[0].astype(bf), row(v_norm_g[0]), w_spatial[0].astype(bf),
               b_spatial[0].T, w_pool[0].astype(bf), b_pool[0], row(pool_scale[0]),
               w_out[0].astype(bf), row(norm2_g[0]), w_router[0].T.astype(bf))

    t_p = x_prompt.shape[0] * x_prompt.shape[1]
    t_s = x_sample.shape[0] * x_sample.shape[1]
    y_all, xn_all, aff = _mixer(x_prompt, x_sample, mixer_w, pad_rows=EXPERT_TM)
    idx_p, gate_p = _route(aff[:, :t_p], 0)
    idx_s, gate_s = _route(aff[:, t_p:], t_p)

    tiles = lambda a: jnp.concatenate([a[0], a[1]], axis=1).reshape(-1, EXPERT_TM)
    assert idx_p.shape[1] % EXPERT_TM == 0 and idx_s.shape[1] % EXPERT_TM == 0
    y_all = _experts(tiles((idx_p, idx_s)), tiles((gate_p, gate_s)),
                     w_gate[0], w_up[0], w_down[0], xn_all, y_all)

    fg = row(final_norm_g)
    return (_final_norm(y_all, fg, 0, t_p).reshape(x_prompt.shape),
            _final_norm(y_all, fg, t_p, t_s).reshape(x_sample.shape))
```

```python
import functools
import math

import jax
import jax.numpy as jnp
from jax import lax
from jax.experimental import pallas as pl
from jax.experimental.pallas import tpu as pltpu

EPS = 1e-6
CHUNK = 128
GMLP_HEADS = 4
POOL_WINDOWS = (2, 4, 8, 16)
HALO = 8
N_EXPERTS = 16
CAPACITY_FACTOR = 2

LANES = 128
BF16_ROWS = 16
MIXER_TM = 256
EXPERT_TM = 1024
EXPERT_TF = 256
ROUTE_W = 512
VMEM_LIMIT = 56 * 1024 * 1024


def _rms(x, g):
    return x * lax.rsqrt(jnp.mean(x * x, axis=-1, keepdims=True) + EPS) * g


def _gelu_tanh(x):
    c = math.sqrt(2.0 / math.pi)
    return 0.5 * x * (1.0 + jnp.tanh(c * (x + 0.044715 * (x * x * x))))


def _dot(a, b):
    return jnp.dot(a, b, preferred_element_type=jnp.float32)


def _to_token_major(ref, x):
    n, rows = x.shape[0], x.shape[1] // LANES
    for k in range(rows):
        ref[pl.ds(k, n, stride=rows), :] = x[:, k * LANES:(k + 1) * LANES]


def _from_token_major(ref, n):
    rows = ref.shape[0] // n
    for k in range(rows):
        yield ref[pl.ds(k, n, stride=rows), :]


def _mixer_kernel(xa_ref, xa_prev_ref, xa_next_ref, xb_ref, xb_prev_ref, xb_next_ref,
                  g1_ref, win_ref, vg_ref, ws_ref, bs_ref, wp_ref, bp_ref, ps_ref, wout_ref,
                  g2_ref, wr_ref, x1_ref, xnorm_ref, aff_ref, mix_ref, *, tiles_a, tiles_b,
                  seq_a, seq_b):
    tm = xa_ref.shape[0]
    gw = vg_ref.shape[1]
    hd = gw // GMLP_HEADS
    pw = ps_ref.shape[1]
    gd = pw // len(POOL_WINDOWS)
    i = pl.program_id(0)
    in_a = i < tiles_a
    seq_len = jnp.where(in_a, seq_a, seq_b)
    tile = jnp.where(in_a, i, jnp.minimum(i - tiles_a, tiles_b - 1))
    s0 = lax.rem(tile * tm, seq_len)

    x = jnp.where(in_a, xa_ref[...], xb_ref[...])
    g1 = g1_ref[...]
    h = _rms(x, g1)
    h_ext = jnp.concatenate([h, _rms(jnp.where(in_a, xa_prev_ref[...], xb_prev_ref[...]), g1),
                             _rms(jnp.where(in_a, xa_next_ref[...], xb_next_ref[...]), g1)], axis=0)
    h_ext = h_ext.astype(jnp.bfloat16)

    z_uv = _gelu_tanh(_dot(h_ext[:tm], win_ref[:, :2 * gw]))
    z_p = _dot(h_ext, win_ref[:, 2 * gw:])

    u = z_uv[:, :gw]
    v = _rms(z_uv[:, gw:], vg_ref[...]).astype(jnp.bfloat16)
    bs = bs_ref[...]
    for c in range(tm // CHUNK):
        rows = slice(c * CHUNK, (c + 1) * CHUNK)
        for hh in range(GMLP_HEADS):
            cols = slice(hh * hd, (hh + 1) * hd)
            sv = _dot(ws_ref[hh], v[rows, cols]) + bs[:, hh:hh + 1]
            mix_ref[rows, cols] = (u[rows, cols] * sv).astype(jnp.bfloat16)

    p = z_p[:tm]
    pe = z_p.astype(jnp.bfloat16)
    ri = lax.broadcasted_iota(jnp.int32, (tm, tm + 2 * HALO), 0) + s0
    q = lax.broadcasted_iota(jnp.int32, (tm, tm + 2 * HALO), 1)
    qpos = jnp.where(q < tm, q, jnp.where(q < tm + HALO, q - tm - HALO, q - HALO)) + s0
    rpos = lax.broadcasted_iota(jnp.int32, (tm, 1), 0) + s0
    for g, w in enumerate(POOL_WINDOWS):
        cols = slice(g * gd, (g + 1) * gd)
        lo = jnp.maximum(ri - w // 2, 0)
        hi = jnp.minimum(ri + w // 2, seq_len)
        band = jnp.where((qpos >= lo) & (qpos < hi), 1.0, 0.0).astype(jnp.bfloat16)
        cnt = jnp.minimum(rpos + w // 2, seq_len) - jnp.maximum(rpos - w // 2, 0)
        d = _dot(band, pe[:, cols]) / cnt.astype(jnp.float32) - p[:, cols]
        yb = _dot(d.astype(jnp.bfloat16), wp_ref[g]) + bp_ref[g:g + 1, :]
        mix_ref[:, gw + g * gd:gw + (g + 1) * gd] = (yb * ps_ref[:, cols]).astype(jnp.bfloat16)

    x1 = x + _dot(mix_ref[...], wout_ref[...])
    x1_ref[...] = x1
    xn = _rms(x1, g2_ref[...])
    xb = xn.astype(jnp.bfloat16)
    words = [pltpu.pack_elementwise([xn[:, 2 * c * LANES:(2 * c + 1) * LANES],
                                     xn[:, (2 * c + 1) * LANES:(2 * c + 2) * LANES]],
                                    packed_dtype=jnp.bfloat16)
             for c in range(xn.shape[1] // (2 * LANES))]
    _to_token_major(xnorm_ref, jnp.concatenate(words, axis=1))
    logits = lax.dot_general(wr_ref[...], xb, (((1,), (1,)), ((), ())),
                             preferred_element_type=jnp.float32)
    e = jnp.exp(logits - jnp.max(logits, axis=0, keepdims=True))
    aff_ref[...] = e / jnp.sum(e, axis=0, keepdims=True)


def _mixer(xa, xb, weights, *, pad_rows):
    d = xa.shape[-1]
    tm = MIXER_TM
    seq_a, seq_b = xa.shape[1], xb.shape[1]
    ta, tb = xa.shape[0] * seq_a, xb.shape[0] * seq_b
    assert seq_a % tm == 0 and seq_b % tm == 0 and tm % CHUNK == 0 and pad_rows % tm == 0
    tiles_a, tiles_b = ta // tm, tb // tm
    n_tiles = tiles_a + tiles_b
    hb = tm // HALO
    xr = d // (2 * LANES)

    def group_specs(first, tiles):
        tile = lambda i: jnp.clip(i - first, 0, tiles - 1)
        last = tiles * hb - 1
        return [pl.BlockSpec((tm, d), lambda i: (tile(i), 0)),
                pl.BlockSpec((HALO, d), lambda i: (jnp.maximum(tile(i) * hb - 1, 0), 0)),
                pl.BlockSpec((HALO, d), lambda i: (jnp.minimum((tile(i) + 1) * hb, last), 0))]
    const = lambda a: pl.BlockSpec(a.shape, lambda i: (0,) * a.ndim, pipeline_mode=pl.Buffered(1))
    real = lambda i: jnp.minimum(i, n_tiles - 1)
    xa2, xb2 = xa.reshape(ta, d), xb.reshape(tb, d)
    return pl.pallas_call(
        functools.partial(_mixer_kernel, tiles_a=tiles_a, tiles_b=tiles_b, seq_a=seq_a, seq_b=seq_b),
        grid=(n_tiles + pad_rows // tm,),
        in_specs=[*group_specs(0, tiles_a), *group_specs(tiles_a, tiles_b),
                  *[const(w) for w in weights]],
        out_specs=[
            pl.BlockSpec((tm, d), lambda i: (i, 0)),
            pl.BlockSpec((tm * xr, LANES), lambda i: (real(i), 0)),
            pl.BlockSpec((N_EXPERTS, tm), lambda i: (0, real(i))),
        ],
        out_shape=[
            jax.ShapeDtypeStruct((ta + tb + pad_rows, d), jnp.float32),
            jax.ShapeDtypeStruct(((ta + tb) * xr, LANES), jnp.uint32),
            jax.ShapeDtypeStruct((N_EXPERTS, ta + tb), jnp.float32),
        ],
        scratch_shapes=[pltpu.VMEM((tm, weights[8].shape[0]), jnp.bfloat16)],
        compiler_params=pltpu.CompilerParams(
            dimension_semantics=("arbitrary",), vmem_limit_bytes=VMEM_LIMIT),
        name="mixer",
    )(xa2, xa2, xa2, xb2, xb2, xb2, *weights)


def _threshold_kernel(a_ref, thr_ref, *, cap):
    ne = a_ref.shape[0]

    def bisect(i, cur):
        cand = cur | jnp.left_shift(jnp.int32(1), 30 - i)
        cand_f = lax.bitcast_convert_type(cand, jnp.float32)
        n = jnp.sum(jnp.where(a_ref[...] >= cand_f, 1.0, 0.0), axis=1, keepdims=True)
        return jnp.where(n >= cap, cand, cur)
    thr = lax.fori_loop(0, 31, bisect, jnp.zeros((ne, 1), jnp.int32))
    thr_ref[...] = jnp.broadcast_to(lax.bitcast_convert_type(thr, jnp.float32), thr_ref.shape)


def _route_kernel(a_ref, thr_ref, idx_ref, gate_ref, *, cap, token_offset):
    a = a_ref[0]
    nb = a.shape[0]
    thr = thr_ref[0]

    def count(mask):
        s = jnp.sum(jnp.where(mask, 1.0, 0.0), axis=0, keepdims=True)
        return jnp.sum(s, axis=1, keepdims=True)

    jj = lax.broadcasted_iota(jnp.int32, (LANES, LANES), 0)
    kk = lax.broadcasted_iota(jnp.int32, (LANES, LANES), 1)
    upper = jnp.where(jj <= kk, 1.0, 0.0).astype(jnp.bfloat16)
    bi = lax.broadcasted_iota(jnp.int32, (nb, nb), 0)
    bk = lax.broadcasted_iota(jnp.int32, (nb, nb), 1)
    lower = jnp.where(bk <= bi, 1.0, 0.0).astype(jnp.bfloat16)

    def ranks(mask):
        m = jnp.where(mask, 1.0, 0.0).astype(jnp.bfloat16)
        local = _dot(m, upper)
        p_in = jnp.sum(_dot(lower, m), axis=1, keepdims=True)
        p_ex = p_in - local[:, LANES - 1:LANES]
        return local, p_in, p_ex

    gt = a > thr
    eq = a == thr
    need = cap - count(gt)
    eq_local, _, eq_pex = ranks(eq)
    sel = gt | (eq & (eq_pex + eq_local <= need))
    local, p_in, p_ex = ranks(sel)

    a_hi = a.astype(jnp.bfloat16)
    r1 = a - a_hi.astype(jnp.float32)
    a_mid = r1.astype(jnp.bfloat16)
    a_lo = (r1 - a_mid.astype(jnp.float32)).astype(jnp.bfloat16)
    local_t = local.T.astype(jnp.bfloat16)
    hi_t = a_hi.astype(jnp.float32).T.astype(jnp.bfloat16)
    mid_t = a_mid.astype(jnp.float32).T.astype(jnp.bfloat16)
    lo_t = a_lo.astype(jnp.float32).T.astype(jnp.bfloat16)
    jcol = lax.broadcasted_iota(jnp.int32, (LANES, 1), 0).astype(jnp.float32)

    w = idx_ref.shape[2]
    def resolve(ci, carry):
        c = (lax.broadcasted_iota(jnp.int32, (1, w), 1) + ci * w).astype(jnp.float32)
        in_blk = (p_ex <= c) & (c < p_in)
        onehot = jnp.where(in_blk, 1.0, 0.0).astype(jnp.bfloat16)
        blk = jnp.sum(jnp.where(p_in <= c, 1.0, 0.0), axis=0, keepdims=True)
        lc = c - jnp.sum(jnp.where(in_blk, p_ex, 0.0), axis=0, keepdims=True)
        g_local = _dot(local_t, onehot)
        jstar = jnp.sum(jnp.where(g_local <= lc, 1.0, 0.0), axis=0, keepdims=True)
        idx_ref[0, pl.ds(ci, 1), :] = (blk * LANES + jstar).astype(jnp.int32) + token_offset
        g_aff = (_dot(hi_t, onehot) + _dot(mid_t, onehot)) + _dot(lo_t, onehot)
        gate_ref[0, pl.ds(ci, 1), :] = jnp.sum(
            jnp.where(jcol == jstar, g_aff, 0.0), axis=0, keepdims=True)
        return carry
    lax.fori_loop(0, cap // w, resolve, 0)


def _route(aff_t, token_offset):
    ne, t = aff_t.shape
    cap = CAPACITY_FACTOR * t // ne
    nb = t // LANES
    w = ROUTE_W
    assert t % LANES == 0 and nb % 8 == 0 and cap % w == 0
    thr = pl.pallas_call(
        functools.partial(_threshold_kernel, cap=cap),
        out_shape=jax.ShapeDtypeStruct((ne, LANES), jnp.float32),
        name="threshold",
    )(aff_t)
    idx, gate = pl.pallas_call(
        functools.partial(_route_kernel, cap=cap, token_offset=token_offset),
        grid=(ne,),
        in_specs=[pl.BlockSpec((1, nb, LANES), lambda e: (e, 0, 0)),
                  pl.BlockSpec((1, 1, LANES), lambda e: (e, 0, 0))],
        out_specs=[pl.BlockSpec((1, cap // w, w), lambda e: (e, 0, 0)),
                   pl.BlockSpec((1, cap // w, w), lambda e: (e, 0, 0))],
        out_shape=[jax.ShapeDtypeStruct((ne, cap // w, w), jnp.int32),
                   jax.ShapeDtypeStruct((ne, cap // w, w), jnp.float32)],
        compiler_params=pltpu.CompilerParams(dimension_semantics=("arbitrary",)),
        name="route",
    )(aff_t.reshape(ne, nb, LANES), thr.reshape(ne, 1, LANES))
    return idx.reshape(ne, cap), gate.reshape(ne, cap)


def _token_copy(hbm, vmem, tok, slot, rows, sem, to_hbm):
    src, dst = hbm.at[pl.ds(tok * rows, rows)], vmem.at[pl.ds(slot * rows, rows)]
    return pltpu.make_async_copy(dst, src, sem) if to_hbm else pltpu.make_async_copy(src, dst, sem)


def _start_rows(hbm, idx_ref, vmem, r0, n, rows, sem, *, to_hbm):
    for r in range(n):
        _token_copy(hbm, vmem, idx_ref[0, 0, r0 + r], r0 + r, rows, sem, to_hbm).start()


def _start_rows_loop(hbm, idx_ref, vmem, n, rows, sem, *, to_hbm):
    def issue(r, c):
        _token_copy(hbm, vmem, idx_ref[0, 0, r], r, rows, sem, to_hbm).start()
        return c
    lax.fori_loop(0, n, issue, 0, unroll=8)


def _wait_rows(hbm, vmem, sem, *, to_hbm):
    rows = hbm.at[pl.ds(0, vmem.shape[0])]
    (pltpu.make_async_copy(vmem, rows, sem) if to_hbm
     else pltpu.make_async_copy(rows, vmem, sem)).wait()


def _expert_kernel(idx_prev, idx_cur, idx_next, gate_prev, gate_cur, wg_ref, wu_ref, wd_ref,
                   xn_hbm, yin_hbm, y_hbm, xbuf, xbf, pool, gcol, sem_x, sem_y, sem_s):
    del yin_hbm
    ybuf, sbuf, acc = pool.at[0], pool.at[1], pool.at[2]
    i = pl.program_id(0)
    f = pl.program_id(1)
    n_tiles = pl.num_programs(0)
    n_steps = pl.num_programs(1)
    tm = acc.shape[0]
    xr = xbuf.shape[1] // tm
    rows_per_step = tm // 8
    slot = i % 2
    pslot = 1 - slot

    def gated_update(gate_ref):
        gcol[...] = jnp.broadcast_to(gate_ref[0], (LANES, tm)).T
        rb = 64
        def rows_pass(b, carry):
            rows = pl.ds(pl.multiple_of(b * rb, rb), rb)
            gate_col = gcol[rows, :]
            for k in range(acc.shape[1] // LANES):
                cols = slice(k * LANES, (k + 1) * LANES)
                sbuf[rows, cols] = ybuf[rows, cols] + acc[rows, cols] * gate_col
            return carry
        lax.fori_loop(0, tm // rb, rows_pass, 0)

    @pl.when((i == 0) & (f == 0))
    def _():
        sbuf[...] = jnp.zeros(sbuf.shape, sbuf.dtype)
        _start_rows_loop(xn_hbm, idx_cur, xbuf.at[0], tm, xr, sem_x.at[0], to_hbm=False)

    @pl.when(f == 0)
    def _():
        @pl.when(i > 0)
        def _():
            _wait_rows(y_hbm, ybuf, sem_y, to_hbm=False)
            _wait_rows(y_hbm, sbuf, sem_s, to_hbm=True)
            gated_update(gate_prev)

        _wait_rows(xn_hbm, xbuf.at[slot], sem_x.at[slot], to_hbm=False)
        for c, w in enumerate(_from_token_major(xbuf.at[slot], tm)):
            lo, hi = (pltpu.unpack_elementwise(w, index=k, packed_dtype=jnp.bfloat16,
                                               unpacked_dtype=jnp.float32) for k in (0, 1))
            xbf[:, 2 * c * LANES:(2 * c + 1) * LANES] = lo.astype(jnp.bfloat16)
            xbf[:, (2 * c + 1) * LANES:(2 * c + 2) * LANES] = hi.astype(jnp.bfloat16)

    r0 = f * rows_per_step
    copies = (
        functools.partial(_start_rows, xn_hbm, idx_next, xbuf.at[pslot], r0, rows_per_step, xr,
                          sem_x.at[pslot], to_hbm=False),
        functools.partial(_start_rows, y_hbm, idx_cur, ybuf, r0, rows_per_step, 1, sem_y,
                          to_hbm=False),
        functools.partial(_start_rows, y_hbm, idx_prev, sbuf, r0, rows_per_step, 1, sem_s,
                          to_hbm=True),
    )
    blocks = len(copies) + 1
    mb = tm // blocks
    wg, wu, wd = (w[...].astype(jnp.bfloat16) for w in (wg_ref, wu_ref, wd_ref))
    for j in range(blocks):
        rows = slice(j * mb, (j + 1) * mb)
        xb = xbf[rows, :]
        g = _dot(xb, wg)
        u = _dot(xb, wu)
        hcol = (g * (1.0 / (1.0 + jnp.exp(-g))) * u).astype(jnp.bfloat16)
        if j < len(copies):
            copies[j]()
        acc[rows, :] = _dot(hcol, wd) + jnp.where(f > 0, acc[rows, :], 0.0)

    @pl.when((i == n_tiles - 1) & (f == n_steps - 1))
    def _():
        _wait_rows(y_hbm, ybuf, sem_y, to_hbm=False)
        _wait_rows(y_hbm, sbuf, sem_s, to_hbm=True)
        _wait_rows(xn_hbm, xbuf.at[pslot], sem_x.at[pslot], to_hbm=False)
        gated_update(gate_cur)
        _start_rows_loop(y_hbm, idx_cur, sbuf, tm, 1, sem_s, to_hbm=True)
        _wait_rows(y_hbm, sbuf, sem_s, to_hbm=True)


def _experts(idx_tiles, gate_tiles, wg, wu, wd, xn_all, y_all):
    ne, d, dff = wg.shape
    tm, tf = EXPERT_TM, EXPERT_TF
    n_tiles = idx_tiles.shape[0]
    per_expert = n_tiles // ne
    n_steps = dff // tf
    xr = d // (2 * LANES)
    n_tok = xn_all.shape[0] // xr
    assert n_steps == 8 and tm % n_steps == 0 and y_all.shape[0] >= n_tok + tm
    rows = jnp.arange(tm, dtype=jnp.int32)[None]
    idx_ext = jnp.concatenate([rows + n_tok, idx_tiles, rows], axis=0)
    gate_ext = jnp.concatenate([jnp.zeros((1, tm), jnp.float32), gate_tiles], axis=0)
    idx_ext = idx_ext.reshape(n_tiles + 2, 1, tm)
    gate_ext = gate_ext.reshape(n_tiles + 1, 1, tm)
    any_spec = pl.BlockSpec(memory_space=pl.ANY)
    idx_spec = lambda k: pl.BlockSpec((1, 1, tm), lambda i, f: (i + k, 0, 0), memory_space=pltpu.SMEM)
    gate_spec = lambda k: pl.BlockSpec((1, 1, tm), lambda i, f: (i + k, 0, 0))
    return pl.pallas_call(
        _expert_kernel,
        grid=(n_tiles, n_steps),
        in_specs=[
            idx_spec(0), idx_spec(1), idx_spec(2), gate_spec(0), gate_spec(1),
            pl.BlockSpec((None, d, tf), lambda i, f: (i // per_expert, 0, f)),
            pl.BlockSpec((None, d, tf), lambda i, f: (i // per_expert, 0, f)),
            pl.BlockSpec((None, tf, d), lambda i, f: (i // per_expert, f, 0)),
            any_spec, any_spec,
        ],
        out_specs=any_spec,
        out_shape=jax.ShapeDtypeStruct(y_all.shape, y_all.dtype),
        input_output_aliases={9: 0},
        scratch_shapes=[
            pltpu.VMEM((2, tm * xr, LANES), xn_all.dtype),
            pltpu.VMEM((tm, d), jnp.bfloat16),
            pltpu.VMEM((3, tm, d), jnp.float32),
            pltpu.VMEM((tm, LANES), jnp.float32),
            pltpu.SemaphoreType.DMA((2,)), pltpu.SemaphoreType.DMA, pltpu.SemaphoreType.DMA,
        ],
        compiler_params=pltpu.CompilerParams(
            dimension_semantics=("arbitrary", "arbitrary"), vmem_limit_bytes=VMEM_LIMIT),
        name="experts",
    )(idx_ext, idx_ext, idx_ext, gate_ext, gate_ext, wg, wu, wd, xn_all, y_all)


def _final_kernel(y_ref, g_ref, o_ref):
    o_ref[...] = _rms(y_ref[...], g_ref[...])


def _final_norm(y_all, g, row_offset, t):
    d = g.shape[1]
    tm = 512
    ob = row_offset // tm
    assert row_offset % tm == 0 and t % tm == 0
    return pl.pallas_call(
        _final_kernel,
        grid=(t // tm,),
        in_specs=[pl.BlockSpec((tm, d), lambda i: (i + ob, 0)),
                  pl.BlockSpec((1, d), lambda i: (0, 0))],
        out_specs=pl.BlockSpec((tm, d), lambda i: (i, 0)),
        out_shape=jax.ShapeDtypeStruct((t, d), jnp.float32),
        compiler_params=pltpu.CompilerParams(dimension_semantics=("arbitrary",)),
        name="final_norm",
    )(y_all, g)


def kernel(x_prompt, x_sample, norm1_g, w_in, v_norm_g, w_spatial, b_spatial, w_pool,
           b_pool, pool_scale, w_out, norm2_g, w_router, w_gate, w_up, w_down, final_norm_g):
    assert norm1_g.shape[0] == 1, "single-layer block"
    bf = jnp.bfloat16
    row = lambda a: a.reshape(1, -1)
    mixer_w = (row(norm1_g[0]), w_in[0].astype(bf), row(v_norm_g[0]), w_spatial[0].astype(bf),
               b_spatial[0].T, w_pool[0].astype(bf), b_pool[0], row(pool_scale[0]),
               w_out[0].astype(bf), row(norm2_g[0]), w_router[0].T.astype(bf))

    t_p = x_prompt.shape[0] * x_prompt.shape[1]
    t_s = x_sample.shape[0] * x_sample.shape[1]
    y_all, xn_all, aff = _mixer(x_prompt, x_sample, mixer_w, pad_rows=EXPERT_TM)
    idx_p, gate_p = _route(aff[:, :t_p], 0)
    idx_s, gate_s = _route(aff[:, t_p:], t_p)

    tiles = lambda a: jnp.concatenate([a[0], a[1]], axis=1).reshape(-1, EXPERT_TM)
    assert idx_p.shape[1] % EXPERT_TM == 0 and idx_s.shape[1] % EXPERT_TM == 0
    y_all = _experts(tiles((idx_p, idx_s)), tiles((gate_p, gate_s)),
                     w_gate[0], w_up[0], w_down[0], xn_all, y_all)

    fg = row(final_norm_g)
    return (_final_norm(y_all, fg, 0, t_p).reshape(x_prompt.shape),
            _final_norm(y_all, fg, t_p, t_s).reshape(x_sample.shape))
```

```python
import functools
import math

import jax
import jax.numpy as jnp
from jax import lax
from jax.experimental import pallas as pl
from jax.experimental.pallas import tpu as pltpu

EPS = 1e-6
CHUNK = 128
GMLP_HEADS = 4
POOL_WINDOWS = (2, 4, 8, 16)
HALO = 8
N_EXPERTS = 16
CAPACITY_FACTOR = 2

LANES = 128
BF16_ROWS = 16
MIXER_TM = 512
EXPERT_TM = 1024
EXPERT_TF = 256
ROUTE_W = 512
VMEM_LIMIT = 56 * 1024 * 1024


def _rms(x, g):
    return x * lax.rsqrt(jnp.mean(x * x, axis=-1, keepdims=True) + EPS) * g


def _gelu_tanh(x):
    c = math.sqrt(2.0 / math.pi)
    return 0.5 * x * (1.0 + jnp.tanh(c * (x + 0.044715 * (x * x * x))))


def _dot(a, b):
    return jnp.dot(a, b, preferred_element_type=jnp.float32)


def _to_token_major(ref, x):
    n, rows = x.shape[0], x.shape[1] // LANES
    for k in range(rows):
        ref[pl.ds(k, n, stride=rows), :] = x[:, k * LANES:(k + 1) * LANES]


def _from_token_major(ref, n):
    rows = ref.shape[0] // n
    for k in range(rows):
        yield ref[pl.ds(k, n, stride=rows), :]


def _mixer_kernel(xa_hbm, xb_hbm, g1_ref, win_ref, vg_ref, ws_ref, bs_ref, wp_ref, bp_ref,
                  ps_ref, wout_ref, g2_ref, wr_ref, x1_ref, xnorm_ref, aff_ref, mix_ref,
                  xbuf, sems, *, tiles_a, tiles_b, seq_a, seq_b):
    tm = x1_ref.shape[0]
    gw = vg_ref.shape[1]
    hd = gw // GMLP_HEADS
    pw = ps_ref.shape[1]
    gd = pw // len(POOL_WINDOWS)
    i = pl.program_id(0)
    in_a = i < tiles_a
    seq_len = jnp.where(in_a, seq_a, seq_b)
    tile = jnp.where(in_a, i, jnp.minimum(i - tiles_a, tiles_b - 1))
    s0 = lax.rem(tile * tm, seq_len)

    def tile_copies(hbm, t, slot):
        n = hbm.shape[0]
        starts = (t * tm, jnp.maximum(t * tm - HALO, 0), jnp.minimum((t + 1) * tm, n - HALO))
        sizes, offs = (tm, HALO, HALO), (0, tm, tm + HALO)
        return [pltpu.make_async_copy(hbm.at[pl.ds(pl.multiple_of(s, HALO), z)],
                                      xbuf.at[slot, pl.ds(o, z)], sems.at[slot, k])
                for k, (s, z, o) in enumerate(zip(starts, sizes, offs))]

    def start_fetch(j, slot):
        @pl.when(j < tiles_a)
        def _():
            for c in tile_copies(xa_hbm, j, slot):
                c.start()

        @pl.when(j >= tiles_a)
        def _():
            for c in tile_copies(xb_hbm, jnp.minimum(j - tiles_a, tiles_b - 1), slot):
                c.start()

    slot = i % 2

    @pl.when(i == 0)
    def _():
        start_fetch(i, slot)

    for c in tile_copies(xa_hbm, 0, slot):
        c.wait()

    @pl.when(i + 1 < pl.num_programs(0))
    def _():
        start_fetch(i + 1, 1 - slot)

    x = xbuf[slot, :tm]
    g1 = g1_ref[...]
    h_ext = _rms(xbuf[slot], g1).astype(jnp.bfloat16)

    z_uv = _gelu_tanh(_dot(h_ext[:tm], win_ref[:, :2 * gw]))
    z_p = _dot(h_ext, win_ref[:, 2 * gw:])

    u = z_uv[:, :gw]
    v = _rms(z_uv[:, gw:], vg_ref[...]).astype(jnp.bfloat16)
    bs = bs_ref[...]
    for c in range(tm // CHUNK):
        rows = slice(c * CHUNK, (c + 1) * CHUNK)
        for hh in range(GMLP_HEADS):
            cols = slice(hh * hd, (hh + 1) * hd)
            sv = _dot(ws_ref[hh], v[rows, cols]) + bs[:, hh:hh + 1]
            mix_ref[rows, cols] = (u[rows, cols] * sv).astype(jnp.bfloat16)

    p = z_p[:tm]
    pe = z_p.astype(jnp.bfloat16)
    ri = lax.broadcasted_iota(jnp.int32, (tm, tm + 2 * HALO), 0) + s0
    q = lax.broadcasted_iota(jnp.int32, (tm, tm + 2 * HALO), 1)
    qpos = jnp.where(q < tm, q, jnp.where(q < tm + HALO, q - tm - HALO, q - HALO)) + s0
    rpos = lax.broadcasted_iota(jnp.int32, (tm, 1), 0) + s0
    for g, w in enumerate(POOL_WINDOWS):
        cols = slice(g * gd, (g + 1) * gd)
        lo = jnp.maximum(ri - w // 2, 0)
        hi = jnp.minimum(ri + w // 2, seq_len)
        band = jnp.where((qpos >= lo) & (qpos < hi), 1.0, 0.0).astype(jnp.bfloat16)
        cnt = jnp.minimum(rpos + w // 2, seq_len) - jnp.maximum(rpos - w // 2, 0)
        d = _dot(band, pe[:, cols]) / cnt.astype(jnp.float32) - p[:, cols]
        yb = _dot(d.astype(jnp.bfloat16), wp_ref[g]) + bp_ref[g:g + 1, :]
        mix_ref[:, gw + g * gd:gw + (g + 1) * gd] = (yb * ps_ref[:, cols]).astype(jnp.bfloat16)

    x1 = x + _dot(mix_ref[...], wout_ref[...])
    x1_ref[...] = x1
    xn = _rms(x1, g2_ref[...])
    xb = xn.astype(jnp.bfloat16)
    words = [pltpu.pack_elementwise([xn[:, 2 * c * LANES:(2 * c + 1) * LANES],
                                     xn[:, (2 * c + 1) * LANES:(2 * c + 2) * LANES]],
                                    packed_dtype=jnp.bfloat16)
             for c in range(xn.shape[1] // (2 * LANES))]
    _to_token_major(xnorm_ref, jnp.concatenate(words, axis=1))
    logits = lax.dot_general(wr_ref[...], xb, (((1,), (1,)), ((), ())),
                             preferred_element_type=jnp.float32)
    e = jnp.exp(logits - jnp.max(logits, axis=0, keepdims=True))
    aff_ref[...] = e / jnp.sum(e, axis=0, keepdims=True)


def _mixer(xa, xb, weights, *, pad_rows):
    d = xa.shape[-1]
    tm = MIXER_TM
    seq_a, seq_b = xa.shape[1], xb.shape[1]
    ta, tb = xa.shape[0] * seq_a, xb.shape[0] * seq_b
    assert seq_a % tm == 0 and seq_b % tm == 0 and tm % CHUNK == 0 and pad_rows % tm == 0
    tiles_a, tiles_b = ta // tm, tb // tm
    n_tiles = tiles_a + tiles_b
    xr = d // (2 * LANES)
    const = lambda a: pl.BlockSpec(a.shape, lambda i: (0,) * a.ndim, pipeline_mode=pl.Buffered(1))
    any_spec = pl.BlockSpec(memory_space=pl.ANY)
    real = lambda i: jnp.minimum(i, n_tiles - 1)
    return pl.pallas_call(
        functools.partial(_mixer_kernel, tiles_a=tiles_a, tiles_b=tiles_b, seq_a=seq_a, seq_b=seq_b),
        grid=(n_tiles + pad_rows // tm,),
        in_specs=[any_spec, any_spec, *[const(w) for w in weights]],
        out_specs=[
            pl.BlockSpec((tm, d), lambda i: (i, 0)),
            pl.BlockSpec((tm * xr, LANES), lambda i: (real(i), 0)),
            pl.BlockSpec((N_EXPERTS, tm), lambda i: (0, real(i))),
        ],
        out_shape=[
            jax.ShapeDtypeStruct((ta + tb + pad_rows, d), jnp.float32),
            jax.ShapeDtypeStruct(((ta + tb) * xr, LANES), jnp.uint32),
            jax.ShapeDtypeStruct((N_EXPERTS, ta + tb), jnp.float32),
        ],
        scratch_shapes=[pltpu.VMEM((tm, weights[8].shape[0]), jnp.bfloat16),
                        pltpu.VMEM((2, tm + 2 * HALO, d), jnp.float32),
                        pltpu.SemaphoreType.DMA((2, 3))],
        compiler_params=pltpu.CompilerParams(
            dimension_semantics=("arbitrary",), vmem_limit_bytes=VMEM_LIMIT),
        name="mixer",
    )(xa.reshape(ta, d), xb.reshape(tb, d), *weights)


def _threshold_kernel(a_ref, thr_ref, *, cap):
    ne = a_ref.shape[0]

    def bisect(i, cur):
        cand = cur | jnp.left_shift(jnp.int32(1), 30 - i)
        cand_f = lax.bitcast_convert_type(cand, jnp.float32)
        n = jnp.sum(jnp.where(a_ref[...] >= cand_f, 1.0, 0.0), axis=1, keepdims=True)
        return jnp.where(n >= cap, cand, cur)
    thr = lax.fori_loop(0, 31, bisect, jnp.zeros((ne, 1), jnp.int32))
    thr_ref[...] = jnp.broadcast_to(lax.bitcast_convert_type(thr, jnp.float32), thr_ref.shape)


def _route_kernel(a_ref, thr_ref, idx_ref, gate_ref, *, cap, token_offset):
    a = a_ref[0]
    nb = a.shape[0]
    thr = thr_ref[0]

    def count(mask):
        s = jnp.sum(jnp.where(mask, 1.0, 0.0), axis=0, keepdims=True)
        return jnp.sum(s, axis=1, keepdims=True)

    jj = lax.broadcasted_iota(jnp.int32, (LANES, LANES), 0)
    kk = lax.broadcasted_iota(jnp.int32, (LANES, LANES), 1)
    upper = jnp.where(jj <= kk, 1.0, 0.0).astype(jnp.bfloat16)
    bi = lax.broadcasted_iota(jnp.int32, (nb, nb), 0)
    bk = lax.broadcasted_iota(jnp.int32, (nb, nb), 1)
    lower = jnp.where(bk <= bi, 1.0, 0.0).astype(jnp.bfloat16)

    def ranks(mask):
        m = jnp.where(mask, 1.0, 0.0).astype(jnp.bfloat16)
        local = _dot(m, upper)
        p_in = jnp.sum(_dot(lower, m), axis=1, keepdims=True)
        p_ex = p_in - local[:, LANES - 1:LANES]
        return local, p_in, p_ex

    gt = a > thr
    eq = a == thr
    need = cap - count(gt)
    eq_local, _, eq_pex = ranks(eq)
    sel = gt | (eq & (eq_pex + eq_local <= need))
    local, p_in, p_ex = ranks(sel)

    a_hi = a.astype(jnp.bfloat16)
    r1 = a - a_hi.astype(jnp.float32)
    a_mid = r1.astype(jnp.bfloat16)
    a_lo = (r1 - a_mid.astype(jnp.float32)).astype(jnp.bfloat16)
    local_t = local.T.astype(jnp.bfloat16)
    hi_t = a_hi.astype(jnp.float32).T.astype(jnp.bfloat16)
    mid_t = a_mid.astype(jnp.float32).T.astype(jnp.bfloat16)
    lo_t = a_lo.astype(jnp.float32).T.astype(jnp.bfloat16)
    jcol = lax.broadcasted_iota(jnp.int32, (LANES, 1), 0).astype(jnp.float32)

    w = idx_ref.shape[2]
    def resolve(ci, carry):
        c = (lax.broadcasted_iota(jnp.int32, (1, w), 1) + ci * w).astype(jnp.float32)
        in_blk = (p_ex <= c) & (c < p_in)
        onehot = jnp.where(in_blk, 1.0, 0.0).astype(jnp.bfloat16)
        blk = jnp.sum(jnp.where(p_in <= c, 1.0, 0.0), axis=0, keepdims=True)
        lc = c - jnp.sum(jnp.where(in_blk, p_ex, 0.0), axis=0, keepdims=True)
        g_local = _dot(local_t, onehot)
        jstar = jnp.sum(jnp.where(g_local <= lc, 1.0, 0.0), axis=0, keepdims=True)
        idx_ref[0, pl.ds(ci, 1), :] = (blk * LANES + jstar).astype(jnp.int32) + token_offset
        g_aff = (_dot(hi_t, onehot) + _dot(mid_t, onehot)) + _dot(lo_t, onehot)
        gate_ref[0, pl.ds(ci, 1), :] = jnp.sum(
            jnp.where(jcol == jstar, g_aff, 0.0), axis=0, keepdims=True)
        return carry
    lax.fori_loop(0, cap // w, resolve, 0)


def _route(aff_t, token_offset):
    ne, t = aff_t.shape
    cap = CAPACITY_FACTOR * t // ne
    nb = t // LANES
    w = ROUTE_W
    assert t % LANES == 0 and nb % 8 == 0 and cap % w == 0
    thr = pl.pallas_call(
        functools.partial(_threshold_kernel, cap=cap),
        out_shape=jax.ShapeDtypeStruct((ne, LANES), jnp.float32),
        name="threshold",
    )(aff_t)
    idx, gate = pl.pallas_call(
        functools.partial(_route_kernel, cap=cap, token_offset=token_offset),
        grid=(ne,),
        in_specs=[pl.BlockSpec((1, nb, LANES), lambda e: (e, 0, 0)),
                  pl.BlockSpec((1, 1, LANES), lambda e: (e, 0, 0))],
        out_specs=[pl.BlockSpec((1, cap // w, w), lambda e: (e, 0, 0)),
                   pl.BlockSpec((1, cap // w, w), lambda e: (e, 0, 0))],
        out_shape=[jax.ShapeDtypeStruct((ne, cap // w, w), jnp.int32),
                   jax.ShapeDtypeStruct((ne, cap // w, w), jnp.float32)],
        compiler_params=pltpu.CompilerParams(dimension_semantics=("arbitrary",)),
        name="route",
    )(aff_t.reshape(ne, nb, LANES), thr.reshape(ne, 1, LANES))
    return idx.reshape(ne, cap), gate.reshape(ne, cap)


def _token_copy(hbm, vmem, tok, slot, rows, sem, to_hbm):
    src, dst = hbm.at[pl.ds(tok * rows, rows)], vmem.at[pl.ds(slot * rows, rows)]
    return pltpu.make_async_copy(dst, src, sem) if to_hbm else pltpu.make_async_copy(src, dst, sem)


ROW_COPY_PRIORITY = 1


def _start_rows(hbm, idx_ref, vmem, r0, n, rows, sem, *, to_hbm):
    for r in range(n):
        _token_copy(hbm, vmem, idx_ref[0, 0, r0 + r], r0 + r, rows, sem, to_hbm).start(
            priority=ROW_COPY_PRIORITY)


def _start_rows_loop(hbm, idx_ref, vmem, n, rows, sem, *, to_hbm):
    def issue(r, c):
        _token_copy(hbm, vmem, idx_ref[0, 0, r], r, rows, sem, to_hbm).start()
        return c
    lax.fori_loop(0, n, issue, 0, unroll=8)


def _wait_rows(hbm, vmem, sem, *, to_hbm):
    rows = hbm.at[pl.ds(0, vmem.shape[0])]
    (pltpu.make_async_copy(vmem, rows, sem) if to_hbm
     else pltpu.make_async_copy(rows, vmem, sem)).wait()


def _expert_kernel(idx_prev, idx_cur, idx_next, gate_prev, gate_cur, wg_ref, wu_ref, wd_ref,
                   xn_hbm, yin_hbm, y_hbm, xbuf, xbf, pool, gcol, sem_x, sem_y, sem_s):
    del yin_hbm
    ybuf, sbuf, acc = pool.at[0], pool.at[1], pool.at[2]
    i = pl.program_id(0)
    f = pl.program_id(1)
    n_tiles = pl.num_programs(0)
    n_steps = pl.num_programs(1)
    tm = acc.shape[0]
    xr = xbuf.shape[1] // tm
    rows_per_step = tm // 8
    slot = i % 2
    pslot = 1 - slot

    def gated_update(gate_ref):
        gcol[...] = jnp.broadcast_to(gate_ref[0], (LANES, tm)).T
        rb = 64
        def rows_pass(b, carry):
            rows = pl.ds(pl.multiple_of(b * rb, rb), rb)
            gate_col = gcol[rows, :]
            for k in range(acc.shape[1] // LANES):
                cols = slice(k * LANES, (k + 1) * LANES)
                sbuf[rows, cols] = ybuf[rows, cols] + acc[rows, cols] * gate_col
            return carry
        lax.fori_loop(0, tm // rb, rows_pass, 0)

    @pl.when((i == 0) & (f == 0))
    def _():
        sbuf[...] = jnp.zeros(sbuf.shape, sbuf.dtype)
        _start_rows_loop(xn_hbm, idx_cur, xbuf.at[0], tm, xr, sem_x.at[0], to_hbm=False)

    @pl.when(f == 0)
    def _():
        @pl.when(i > 0)
        def _():
            _wait_rows(y_hbm, ybuf, sem_y, to_hbm=False)
            _wait_rows(y_hbm, sbuf, sem_s, to_hbm=True)
            gated_update(gate_prev)

        _wait_rows(xn_hbm, xbuf.at[slot], sem_x.at[slot], to_hbm=False)
        for c, w in enumerate(_from_token_major(xbuf.at[slot], tm)):
            lo, hi = (pltpu.unpack_elementwise(w, index=k, packed_dtype=jnp.bfloat16,
                                               unpacked_dtype=jnp.float32) for k in (0, 1))
            xbf[:, 2 * c * LANES:(2 * c + 1) * LANES] = lo.astype(jnp.bfloat16)
            xbf[:, (2 * c + 1) * LANES:(2 * c + 2) * LANES] = hi.astype(jnp.bfloat16)

    r0 = f * rows_per_step
    copies = (
        functools.partial(_start_rows, xn_hbm, idx_next, xbuf.at[pslot], r0, rows_per_step, xr,
                          sem_x.at[pslot], to_hbm=False),
        functools.partial(_start_rows, y_hbm, idx_cur, ybuf, r0, rows_per_step, 1, sem_y,
                          to_hbm=False),
        functools.partial(_start_rows, y_hbm, idx_prev, sbuf, r0, rows_per_step, 1, sem_s,
                          to_hbm=True),
    )
    blocks = len(copies) + 1
    mb = tm // blocks
    wg, wu, wd = (w[...].astype(jnp.bfloat16) for w in (wg_ref, wu_ref, wd_ref))
    for j in range(blocks):
        rows = slice(j * mb, (j + 1) * mb)
        xb = xbf[rows, :]
        g = _dot(xb, wg)
        u = _dot(xb, wu)
        hcol = (g * (1.0 / (1.0 + jnp.exp(-g))) * u).astype(jnp.bfloat16)
        if j < len(copies):
            copies[j]()
        acc[rows, :] = _dot(hcol, wd) + jnp.where(f > 0, acc[rows, :], 0.0)

    @pl.when((i == n_tiles - 1) & (f == n_steps - 1))
    def _():
        _wait_rows(y_hbm, ybuf, sem_y, to_hbm=False)
        _wait_rows(y_hbm, sbuf, sem_s, to_hbm=True)
        _wait_rows(xn_hbm, xbuf.at[pslot], sem_x.at[pslot], to_hbm=False)
        gated_update(gate_cur)
        _start_rows_loop(y_hbm, idx_cur, sbuf, tm, 1, sem_s, to_hbm=True)
        _wait_rows(y_hbm, sbuf, sem_s, to_hbm=True)


def _experts(idx_tiles, gate_tiles, wg, wu, wd, xn_all, y_all):
    ne, d, dff = wg.shape
    tm, tf = EXPERT_TM, EXPERT_TF
    n_tiles = idx_tiles.shape[0]
    per_expert = n_tiles // ne
    n_steps = dff // tf
    xr = d // (2 * LANES)
    n_tok = xn_all.shape[0] // xr
    assert n_steps == 8 and tm % n_steps == 0 and y_all.shape[0] >= n_tok + tm
    rows = jnp.arange(tm, dtype=jnp.int32)[None]
    idx_ext = jnp.concatenate([rows + n_tok, idx_tiles, rows], axis=0)
    gate_ext = jnp.concatenate([jnp.zeros((1, tm), jnp.float32), gate_tiles], axis=0)
    idx_ext = idx_ext.reshape(n_tiles + 2, 1, tm)
    gate_ext = gate_ext.reshape(n_tiles + 1, 1, tm)
    any_spec = pl.BlockSpec(memory_space=pl.ANY)
    idx_spec = lambda k: pl.BlockSpec((1, 1, tm), lambda i, f: (i + k, 0, 0), memory_space=pltpu.SMEM)
    gate_spec = lambda k: pl.BlockSpec((1, 1, tm), lambda i, f: (i + k, 0, 0))
    return pl.pallas_call(
        _expert_kernel,
        grid=(n_tiles, n_steps),
        in_specs=[
            idx_spec(0), idx_spec(1), idx_spec(2), gate_spec(0), gate_spec(1),
            pl.BlockSpec((None, d, tf), lambda i, f: (i // per_expert, 0, f)),
            pl.BlockSpec((None, d, tf), lambda i, f: (i // per_expert, 0, f)),
            pl.BlockSpec((None, tf, d), lambda i, f: (i // per_expert, f, 0)),
            any_spec, any_spec,
        ],
        out_specs=any_spec,
        out_shape=jax.ShapeDtypeStruct(y_all.shape, y_all.dtype),
        input_output_aliases={9: 0},
        scratch_shapes=[
            pltpu.VMEM((2, tm * xr, LANES), xn_all.dtype),
            pltpu.VMEM((tm, d), jnp.bfloat16),
            pltpu.VMEM((3, tm, d), jnp.float32),
            pltpu.VMEM((tm, LANES), jnp.float32),
            pltpu.SemaphoreType.DMA((2,)), pltpu.SemaphoreType.DMA, pltpu.SemaphoreType.DMA,
        ],
        compiler_params=pltpu.CompilerParams(
            dimension_semantics=("arbitrary", "arbitrary"), vmem_limit_bytes=VMEM_LIMIT),
        name="experts",
    )(idx_ext, idx_ext, idx_ext, gate_ext, gate_ext, wg, wu, wd, xn_all, y_all)


def _final_kernel(y_ref, g_ref, o_ref):
    o_ref[...] = _rms(y_ref[...], g_ref[...])


def _final_norm(y_all, g, row_offset, t):
    d = g.shape[1]
    tm = 512
    ob = row_offset // tm
    assert row_offset % tm == 0 and t % tm == 0
    return pl.pallas_call(
        _final_kernel,
        grid=(t // tm,),
        in_specs=[pl.BlockSpec((tm, d), lambda i: (i + ob, 0)),
                  pl.BlockSpec((1, d), lambda i: (0, 0))],
        out_specs=pl.BlockSpec((tm, d), lambda i: (i, 0)),
        out_shape=jax.ShapeDtypeStruct((t, d), jnp.float32),
        compiler_params=pltpu.CompilerParams(dimension_semantics=("arbitrary",)),
        name="final_norm",
    )(y_all, g)


def kernel(x_prompt, x_sample, norm1_g, w_in, v_norm_g, w_spatial, b_spatial, w_pool,
           b_pool, pool_scale, w_out, norm2_g, w_router, w_gate, w_up, w_down, final_norm_g):
    assert norm1_g.shape[0] == 1, "single-layer block"
    bf = jnp.bfloat16
    row = lambda a: a.reshape(1, -1)
    mixer_w = (row(norm1_g[0]), w_in[0].astype(bf), row(v_norm_g[0]), w_spatial[0].astype(bf),
               b_spatial[0].T, w_pool[0].astype(bf), b_pool[0], row(pool_scale[0]),
               w_out[0].astype(bf), row(norm2_g[0]), w_router[0].T.astype(bf))

    t_p = x_prompt.shape[0] * x_prompt.shape[1]
    t_s = x_sample.shape[0] * x_sample.shape[1]
    y_all, xn_all, aff = _mixer(x_prompt, x_sample, mixer_w, pad_rows=EXPERT_TM)
    idx_p, gate_p = _route(aff[:, :t_p], 0)
    idx_s, gate_s = _route(aff[:, t_p:], t_p)

    tiles = lambda a: jnp.concatenate([a[0], a[1]], axis=1).reshape(-1, EXPERT_TM)
    assert idx_p.shape[1] % EXPERT_TM == 0 and idx_s.shape[1] % EXPERT_TM == 0
    y_all = _experts(tiles((idx_p, idx_s)), tiles((gate_p, gate_s)),
                     w_gate[0], w_up[0], w_down[0], xn_all, y_all)

    fg = row(final_norm_g)
    return (_final_norm(y_all, fg, 0, t_p).reshape(x_prompt.shape),
            _final_norm(y_all, fg, t_p, t_s).reshape(x_sample.shape))
```

```python
import functools
import math

import jax
import jax.numpy as jnp
from jax import lax
from jax.experimental import pallas as pl
from jax.experimental.pallas import tpu as pltpu

EPS = 1e-6
CHUNK = 128
GMLP_HEADS = 4
POOL_WINDOWS = (2, 4, 8, 16)
HALO = 8
N_EXPERTS = 16
CAPACITY_FACTOR = 2

LANES = 128
BF16_ROWS = 16
MIXER_TM = 512
EXPERT_TM = 1024
EXPERT_TF = 256
ROUTE_W = 512
VMEM_LIMIT = 56 * 1024 * 1024


def _rms(x, g):
    return x * lax.rsqrt(jnp.mean(x * x, axis=-1, keepdims=True) + EPS) * g


def _gelu_tanh(x):
    c = math.sqrt(2.0 / math.pi)
    return 0.5 * x * (1.0 + jnp.tanh(c * (x + 0.044715 * (x * x * x))))


def _dot(a, b):
    return jnp.dot(a, b, preferred_element_type=jnp.float32)


def _to_token_major(ref, x):
    n, rows = x.shape[0], x.shape[1] // LANES
    for k in range(rows):
        ref[pl.ds(k, n, stride=rows), :] = x[:, k * LANES:(k + 1) * LANES]


def _from_token_major(ref, n):
    rows = ref.shape[0] // n
    for k in range(rows):
        yield ref[pl.ds(k, n, stride=rows), :]


def _mixer_kernel(xa_hbm, xb_hbm, g1_ref, win_ref, vg_ref, ws_ref, bs_ref, wp_ref, bp_ref,
                  ps_ref, wout_ref, g2_ref, wr_ref, x1_ref, xnorm_ref, aff_ref, mix_ref,
                  xbuf, sems, *, tiles_a, tiles_b, seq_a, seq_b):
    tm = x1_ref.shape[0]
    gw = vg_ref.shape[1]
    hd = gw // GMLP_HEADS
    pw = ps_ref.shape[1]
    gd = pw // len(POOL_WINDOWS)
    i = pl.program_id(0)
    in_a = i < tiles_a
    seq_len = jnp.where(in_a, seq_a, seq_b)
    tile = jnp.where(in_a, i, jnp.minimum(i - tiles_a, tiles_b - 1))
    s0 = lax.rem(tile * tm, seq_len)

    def tile_copies(hbm, t, slot):
        n = hbm.shape[0]
        starts = (t * tm, jnp.maximum(t * tm - HALO, 0), jnp.minimum((t + 1) * tm, n - HALO))
        sizes, offs = (tm, HALO, HALO), (0, tm, tm + HALO)
        return [pltpu.make_async_copy(hbm.at[pl.ds(pl.multiple_of(s, HALO), z)],
                                      xbuf.at[slot, pl.ds(o, z)], sems.at[slot, k])
                for k, (s, z, o) in enumerate(zip(starts, sizes, offs))]

    def start_fetch(j, slot):
        @pl.when(j < tiles_a)
        def _():
            for c in tile_copies(xa_hbm, j, slot):
                c.start()

        @pl.when(j >= tiles_a)
        def _():
            for c in tile_copies(xb_hbm, jnp.minimum(j - tiles_a, tiles_b - 1), slot):
                c.start()

    slot = i % 2

    @pl.when(i == 0)
    def _():
        start_fetch(i, slot)

    for c in tile_copies(xa_hbm, 0, slot):
        c.wait()

    @pl.when(i + 1 < pl.num_programs(0))
    def _():
        start_fetch(i + 1, 1 - slot)

    x = xbuf[slot, :tm]
    g1 = g1_ref[...]
    h_ext = _rms(xbuf[slot], g1).astype(jnp.bfloat16)

    z_uv = _gelu_tanh(_dot(h_ext[:tm], win_ref[:, :2 * gw]))
    z_p = _dot(h_ext, win_ref[:, 2 * gw:])

    u = z_uv[:, :gw]
    v = _rms(z_uv[:, gw:], vg_ref[...]).astype(jnp.bfloat16)
    bs = bs_ref[...]
    for c in range(tm // CHUNK):
        rows = slice(c * CHUNK, (c + 1) * CHUNK)
        for hh in range(GMLP_HEADS):
            cols = slice(hh * hd, (hh + 1) * hd)
            sv = _dot(ws_ref[hh], v[rows, cols]) + bs[:, hh:hh + 1]
            mix_ref[rows, cols] = (u[rows, cols] * sv).astype(jnp.bfloat16)

    p = z_p[:tm]
    pe = z_p.astype(jnp.bfloat16)
    ri = lax.broadcasted_iota(jnp.int32, (tm, tm + 2 * HALO), 0) + s0
    q = lax.broadcasted_iota(jnp.int32, (tm, tm + 2 * HALO), 1)
    qpos = jnp.where(q < tm, q, jnp.where(q < tm + HALO, q - tm - HALO, q - HALO)) + s0
    rpos = lax.broadcasted_iota(jnp.int32, (tm, 1), 0) + s0
    for g, w in enumerate(POOL_WINDOWS):
        cols = slice(g * gd, (g + 1) * gd)
        lo = jnp.maximum(ri - w // 2, 0)
        hi = jnp.minimum(ri + w // 2, seq_len)
        band = jnp.where((qpos >= lo) & (qpos < hi), 1.0, 0.0).astype(jnp.bfloat16)
        cnt = jnp.minimum(rpos + w // 2, seq_len) - jnp.maximum(rpos - w // 2, 0)
        d = _dot(band, pe[:, cols]) / cnt.astype(jnp.float32) - p[:, cols]
        yb = _dot(d.astype(jnp.bfloat16), wp_ref[g]) + bp_ref[g:g + 1, :]
        mix_ref[:, gw + g * gd:gw + (g + 1) * gd] = (yb * ps_ref[:, cols]).astype(jnp.bfloat16)

    x1 = x + _dot(mix_ref[...], wout_ref[...])
    x1_ref[...] = x1
    xn = _rms(x1, g2_ref[...])
    xb = xn.astype(jnp.bfloat16)
    words = [pltpu.pack_elementwise([xn[:, 2 * c * LANES:(2 * c + 1) * LANES],
                                     xn[:, (2 * c + 1) * LANES:(2 * c + 2) * LANES]],
                                    packed_dtype=jnp.bfloat16)
             for c in range(xn.shape[1] // (2 * LANES))]
    _to_token_major(xnorm_ref, jnp.concatenate(words, axis=1))
    logits = lax.dot_general(wr_ref[...], xb, (((1,), (1,)), ((), ())),
                             preferred_element_type=jnp.float32)
    e = jnp.exp(logits - jnp.max(logits, axis=0, keepdims=True))
    aff_ref[...] = e / jnp.sum(e, axis=0, keepdims=True)


def _mixer(xa, xb, weights, *, pad_rows):
    d = xa.shape[-1]
    tm = MIXER_TM
    seq_a, seq_b = xa.shape[1], xb.shape[1]
    ta, tb = xa.shape[0] * seq_a, xb.shape[0] * seq_b
    assert seq_a % tm == 0 and seq_b % tm == 0 and tm % CHUNK == 0 and pad_rows % tm == 0
    tiles_a, tiles_b = ta // tm, tb // tm
    n_tiles = tiles_a + tiles_b
    xr = d // (2 * LANES)
    const = lambda a: pl.BlockSpec(a.shape, lambda i: (0,) * a.ndim, pipeline_mode=pl.Buffered(1))
    any_spec = pl.BlockSpec(memory_space=pl.ANY)
    real = lambda i: jnp.minimum(i, n_tiles - 1)
    return pl.pallas_call(
        functools.partial(_mixer_kernel, tiles_a=tiles_a, tiles_b=tiles_b, seq_a=seq_a, seq_b=seq_b),
        grid=(n_tiles + pad_rows // tm,),
        in_specs=[any_spec, any_spec, *[const(w) for w in weights]],
        out_specs=[
            pl.BlockSpec((tm, d), lambda i: (i, 0)),
            pl.BlockSpec((tm * xr, LANES), lambda i: (real(i), 0)),
            pl.BlockSpec((N_EXPERTS, tm), lambda i: (0, real(i))),
        ],
        out_shape=[
            jax.ShapeDtypeStruct((ta + tb + pad_rows, d), jnp.float32),
            jax.ShapeDtypeStruct(((ta + tb) * xr, LANES), jnp.uint32),
            jax.ShapeDtypeStruct((N_EXPERTS, ta + tb), jnp.float32),
        ],
        scratch_shapes=[pltpu.VMEM((tm, weights[8].shape[0]), jnp.bfloat16),
                        pltpu.VMEM((2, tm + 2 * HALO, d), jnp.float32),
                        pltpu.SemaphoreType.DMA((2, 3))],
        compiler_params=pltpu.CompilerParams(
            dimension_semantics=("arbitrary",), vmem_limit_bytes=VMEM_LIMIT),
        name="mixer",
    )(xa.reshape(ta, d), xb.reshape(tb, d), *weights)


def _threshold_kernel(a_ref, thr_ref, *, cap):
    ne = a_ref.shape[0]

    def bisect(i, cur):
        cand = cur | jnp.left_shift(jnp.int32(1), 30 - i)
        cand_f = lax.bitcast_convert_type(cand, jnp.float32)
        n = jnp.sum(jnp.where(a_ref[...] >= cand_f, 1.0, 0.0), axis=1, keepdims=True)
        return jnp.where(n >= cap, cand, cur)
    thr = lax.fori_loop(0, 31, bisect, jnp.zeros((ne, 1), jnp.int32))
    thr_ref[...] = jnp.broadcast_to(lax.bitcast_convert_type(thr, jnp.float32), thr_ref.shape)


def _route_kernel(a_ref, thr_ref, idx_ref, gate_ref, *, cap, token_offset):
    a = a_ref[0]
    nb = a.shape[0]
    thr = thr_ref[0]

    def count(mask):
        s = jnp.sum(jnp.where(mask, 1.0, 0.0), axis=0, keepdims=True)
        return jnp.sum(s, axis=1, keepdims=True)

    jj = lax.broadcasted_iota(jnp.int32, (LANES, LANES), 0)
    kk = lax.broadcasted_iota(jnp.int32, (LANES, LANES), 1)
    upper = jnp.where(jj <= kk, 1.0, 0.0).astype(jnp.bfloat16)
    bi = lax.broadcasted_iota(jnp.int32, (nb, nb), 0)
    bk = lax.broadcasted_iota(jnp.int32, (nb, nb), 1)
    lower = jnp.where(bk <= bi, 1.0, 0.0).astype(jnp.bfloat16)

    def ranks(mask):
        m = jnp.where(mask, 1.0, 0.0).astype(jnp.bfloat16)
        local = _dot(m, upper)
        p_in = jnp.sum(_dot(lower, m), axis=1, keepdims=True)
        p_ex = p_in - local[:, LANES - 1:LANES]
        return local, p_in, p_ex

    gt = a > thr
    eq = a == thr
    need = cap - count(gt)
    eq_local, _, eq_pex = ranks(eq)
    sel = gt | (eq & (eq_pex + eq_local <= need))
    local, p_in, p_ex = ranks(sel)

    a_hi = a.astype(jnp.bfloat16)
    r1 = a - a_hi.astype(jnp.float32)
    a_mid = r1.astype(jnp.bfloat16)
    a_lo = (r1 - a_mid.astype(jnp.float32)).astype(jnp.bfloat16)
    local_t = local.T.astype(jnp.bfloat16)
    hi_t = a_hi.astype(jnp.float32).T.astype(jnp.bfloat16)
    mid_t = a_mid.astype(jnp.float32).T.astype(jnp.bfloat16)
    lo_t = a_lo.astype(jnp.float32).T.astype(jnp.bfloat16)
    jcol = lax.broadcasted_iota(jnp.int32, (LANES, 1), 0).astype(jnp.float32)

    w = idx_ref.shape[2]
    def resolve(ci, carry):
        c = (lax.broadcasted_iota(jnp.int32, (1, w), 1) + ci * w).astype(jnp.float32)
        in_blk = (p_ex <= c) & (c < p_in)
        onehot = jnp.where(in_blk, 1.0, 0.0).astype(jnp.bfloat16)
        blk = jnp.sum(jnp.where(p_in <= c, 1.0, 0.0), axis=0, keepdims=True)
        lc = c - jnp.sum(jnp.where(in_blk, p_ex, 0.0), axis=0, keepdims=True)
        g_local = _dot(local_t, onehot)
        jstar = jnp.sum(jnp.where(g_local <= lc, 1.0, 0.0), axis=0, keepdims=True)
        idx_ref[0, pl.ds(ci, 1), :] = (blk * LANES + jstar).astype(jnp.int32) + token_offset
        g_aff = (_dot(hi_t, onehot) + _dot(mid_t, onehot)) + _dot(lo_t, onehot)
        gate_ref[0, pl.ds(ci, 1), :] = jnp.sum(
            jnp.where(jcol == jstar, g_aff, 0.0), axis=0, keepdims=True)
        return carry
    lax.fori_loop(0, cap // w, resolve, 0)


def _route(aff_t, token_offset):
    ne, t = aff_t.shape
    cap = CAPACITY_FACTOR * t // ne
    nb = t // LANES
    w = ROUTE_W
    assert t % LANES == 0 and nb % 8 == 0 and cap % w == 0
    thr = pl.pallas_call(
        functools.partial(_threshold_kernel, cap=cap),
        out_shape=jax.ShapeDtypeStruct((ne, LANES), jnp.float32),
        name="threshold",
    )(aff_t)
    idx, gate = pl.pallas_call(
        functools.partial(_route_kernel, cap=cap, token_offset=token_offset),
        grid=(ne,),
        in_specs=[pl.BlockSpec((1, nb, LANES), lambda e: (e, 0, 0)),
                  pl.BlockSpec((1, 1, LANES), lambda e: (e, 0, 0))],
        out_specs=[pl.BlockSpec((1, cap // w, w), lambda e: (e, 0, 0)),
                   pl.BlockSpec((1, cap // w, w), lambda e: (e, 0, 0))],
        out_shape=[jax.ShapeDtypeStruct((ne, cap // w, w), jnp.int32),
                   jax.ShapeDtypeStruct((ne, cap // w, w), jnp.float32)],
        compiler_params=pltpu.CompilerParams(dimension_semantics=("arbitrary",)),
        name="route",
    )(aff_t.reshape(ne, nb, LANES), thr.reshape(ne, 1, LANES))
    return idx.reshape(ne, cap), gate.reshape(ne, cap)


def _token_copy(hbm, vmem, tok, slot, rows, sem, to_hbm):
    src, dst = hbm.at[pl.ds(tok * rows, rows)], vmem.at[pl.ds(slot * rows, rows)]
    return pltpu.make_async_copy(dst, src, sem) if to_hbm else pltpu.make_async_copy(src, dst, sem)


ROW_COPY_PRIORITY = 1


def _start_rows(hbm, idx_ref, vmem, r0, n, rows, sem, *, to_hbm):
    for r in range(n):
        _token_copy(hbm, vmem, idx_ref[0, 0, r0 + r], r0 + r, rows, sem, to_hbm).start(
            priority=ROW_COPY_PRIORITY)


def _start_rows_loop(hbm, idx_ref, vmem, n, rows, sem, *, to_hbm):
    def issue(r, c):
        _token_copy(hbm, vmem, idx_ref[0, 0, r], r, rows, sem, to_hbm).start()
        return c
    lax.fori_loop(0, n, issue, 0, unroll=8)


def _wait_rows(hbm, vmem, sem, *, to_hbm):
    rows = hbm.at[pl.ds(0, vmem.shape[0])]
    (pltpu.make_async_copy(vmem, rows, sem) if to_hbm
     else pltpu.make_async_copy(rows, vmem, sem)).wait()


def _expert_kernel(idx_prev, idx_cur, idx_next, gate_prev, gate_cur, wg_ref, wu_ref, wd_ref,
                   xn_hbm, yin_hbm, y_hbm, xbuf, xbf, pool, gcol, sem_x, sem_y, sem_s):
    del yin_hbm
    ybuf, sbuf, acc = pool.at[0], pool.at[1], pool.at[2]
    i = pl.program_id(0)
    f = pl.program_id(1)
    n_tiles = pl.num_programs(0)
    n_steps = pl.num_programs(1)
    tm = acc.shape[0]
    xr = xbuf.shape[1] // tm
    rows_per_step = tm * wd_ref.shape[0] // wd_ref.shape[1]
    slot = i % 2
    pslot = 1 - slot

    def gated_update(gate_ref):
        gcol[...] = jnp.broadcast_to(gate_ref[0], (LANES, tm)).T
        rb = 64
        def rows_pass(b, carry):
            rows = pl.ds(pl.multiple_of(b * rb, rb), rb)
            gate_col = gcol[rows, :]
            for k in range(acc.shape[1] // LANES):
                cols = slice(k * LANES, (k + 1) * LANES)
                sbuf[rows, cols] = ybuf[rows, cols] + acc[rows, cols] * gate_col
            return carry
        lax.fori_loop(0, tm // rb, rows_pass, 0)

    @pl.when((i == 0) & (f == 0))
    def _():
        sbuf[...] = jnp.zeros(sbuf.shape, sbuf.dtype)
        _start_rows_loop(xn_hbm, idx_cur, xbuf.at[0], tm, xr, sem_x.at[0], to_hbm=False)

    @pl.when(f == 0)
    def _():
        @pl.when(i > 0)
        def _():
            _wait_rows(y_hbm, ybuf, sem_y, to_hbm=False)
            _wait_rows(y_hbm, sbuf, sem_s, to_hbm=True)
            gated_update(gate_prev)

        _wait_rows(xn_hbm, xbuf.at[slot], sem_x.at[slot], to_hbm=False)
        for c, w in enumerate(_from_token_major(xbuf.at[slot], tm)):
            lo, hi = (pltpu.unpack_elementwise(w, index=k, packed_dtype=jnp.bfloat16,
                                               unpacked_dtype=jnp.float32) for k in (0, 1))
            xbf[:, 2 * c * LANES:(2 * c + 1) * LANES] = lo.astype(jnp.bfloat16)
            xbf[:, (2 * c + 1) * LANES:(2 * c + 2) * LANES] = hi.astype(jnp.bfloat16)

    r0 = f * rows_per_step
    copies = (
        functools.partial(_start_rows, xn_hbm, idx_next, xbuf.at[pslot], r0, rows_per_step, xr,
                          sem_x.at[pslot], to_hbm=False),
        functools.partial(_start_rows, y_hbm, idx_cur, ybuf, r0, rows_per_step, 1, sem_y,
                          to_hbm=False),
        functools.partial(_start_rows, y_hbm, idx_prev, sbuf, r0, rows_per_step, 1, sem_s,
                          to_hbm=True),
    )
    blocks = len(copies) + 1
    mb = tm // blocks
    wg, wu, wd = wg_ref[...], wu_ref[...], wd_ref[...]
    for j in range(blocks):
        rows = slice(j * mb, (j + 1) * mb)
        xb = xbf[rows, :]
        g = _dot(xb, wg)
        u = _dot(xb, wu)
        hcol = (g * (1.0 / (1.0 + jnp.exp(-g))) * u).astype(jnp.bfloat16)
        if j < len(copies):
            copies[j]()
        acc[rows, :] = _dot(hcol, wd) + jnp.where(f > 0, acc[rows, :], 0.0)

    @pl.when((i == n_tiles - 1) & (f == n_steps - 1))
    def _():
        _wait_rows(y_hbm, ybuf, sem_y, to_hbm=False)
        _wait_rows(y_hbm, sbuf, sem_s, to_hbm=True)
        _wait_rows(xn_hbm, xbuf.at[pslot], sem_x.at[pslot], to_hbm=False)
        gated_update(gate_cur)
        _start_rows_loop(y_hbm, idx_cur, sbuf, tm, 1, sem_s, to_hbm=True)
        _wait_rows(y_hbm, sbuf, sem_s, to_hbm=True)


def _experts(idx_tiles, gate_tiles, wg, wu, wd, xn_all, y_all):
    ne, d, dff = wg.shape
    tm, tf = EXPERT_TM, EXPERT_TF
    n_tiles = idx_tiles.shape[0]
    per_expert = n_tiles // ne
    n_steps = dff // tf
    xr = d // (2 * LANES)
    n_tok = xn_all.shape[0] // xr
    assert tm % (4 * n_steps) == 0 and d == dff and wg.dtype == jnp.bfloat16 and y_all.shape[0] >= n_tok + tm
    rows = jnp.arange(tm, dtype=jnp.int32)[None]
    idx_ext = jnp.concatenate([rows + n_tok, idx_tiles, rows], axis=0)
    gate_ext = jnp.concatenate([jnp.zeros((1, tm), jnp.float32), gate_tiles], axis=0)
    idx_ext = idx_ext.reshape(n_tiles + 2, 1, tm)
    gate_ext = gate_ext.reshape(n_tiles + 1, 1, tm)
    any_spec = pl.BlockSpec(memory_space=pl.ANY)
    idx_spec = lambda k: pl.BlockSpec((1, 1, tm), lambda i, f: (i + k, 0, 0), memory_space=pltpu.SMEM)
    gate_spec = lambda k: pl.BlockSpec((1, 1, tm), lambda i, f: (i + k, 0, 0))
    return pl.pallas_call(
        _expert_kernel,
        grid=(n_tiles, n_steps),
        in_specs=[
            idx_spec(0), idx_spec(1), idx_spec(2), gate_spec(0), gate_spec(1),
            pl.BlockSpec((None, d, tf), lambda i, f: (i // per_expert, 0, f)),
            pl.BlockSpec((None, d, tf), lambda i, f: (i // per_expert, 0, f)),
            pl.BlockSpec((None, tf, d), lambda i, f: (i // per_expert, f, 0)),
            any_spec, any_spec,
        ],
        out_specs=any_spec,
        out_shape=jax.ShapeDtypeStruct(y_all.shape, y_all.dtype),
        input_output_aliases={9: 0},
        scratch_shapes=[
            pltpu.VMEM((2, tm * xr, LANES), xn_all.dtype),
            pltpu.VMEM((tm, d), jnp.bfloat16),
            pltpu.VMEM((3, tm, d), jnp.float32),
            pltpu.VMEM((tm, LANES), jnp.float32),
            pltpu.SemaphoreType.DMA((2,)), pltpu.SemaphoreType.DMA, pltpu.SemaphoreType.DMA,
        ],
        compiler_params=pltpu.CompilerParams(
            dimension_semantics=("arbitrary", "arbitrary"), vmem_limit_bytes=VMEM_LIMIT),
        name="experts",
    )(idx_ext, idx_ext, idx_ext, gate_ext, gate_ext, wg, wu, wd, xn_all, y_all)


def _final_kernel(y_ref, g_ref, o_ref):
    o_ref[...] = _rms(y_ref[...], g_ref[...])


def _final_norm(y_all, g, row_offset, t):
    d = g.shape[1]
    tm = 512
    ob = row_offset // tm
    assert row_offset % tm == 0 and t % tm == 0
    return pl.pallas_call(
        _final_kernel,
        grid=(t // tm,),
        in_specs=[pl.BlockSpec((tm, d), lambda i: (i + ob, 0)),
                  pl.BlockSpec((1, d), lambda i: (0, 0))],
        out_specs=pl.BlockSpec((tm, d), lambda i: (i, 0)),
        out_shape=jax.ShapeDtypeStruct((t, d), jnp.float32),
        compiler_params=pltpu.CompilerParams(dimension_semantics=("arbitrary",)),
        name="final_norm",
    )(y_all, g)


def kernel(x_prompt, x_sample, norm1_g, w_in, v_norm_g, w_spatial, b_spatial, w_pool,
           b_pool, pool_scale, w_out, norm2_g, w_router, w_gate, w_up, w_down, final_norm_g):
    assert norm1_g.shape[0] == 1, "single-layer block"
    bf = jnp.bfloat16
    row = lambda a: a.reshape(1, -1)
    mixer_w = (row(norm1_g[0]), w_in[0].astype(bf), row(v_norm_g[0]), w_spatial[0].astype(bf),
               b_spatial[0].T, w_pool[0].astype(bf), b_pool[0], row(pool_scale[0]),
               w_out[0].astype(bf), row(norm2_g[0]), w_router[0].T.astype(bf))

    t_p = x_prompt.shape[0] * x_prompt.shape[1]
    t_s = x_sample.shape[0] * x_sample.shape[1]
    y_all, xn_all, aff = _mixer(x_prompt, x_sample, mixer_w, pad_rows=EXPERT_TM)
    idx_p, gate_p = _route(aff[:, :t_p], 0)
    idx_s, gate_s = _route(aff[:, t_p:], t_p)

    tiles = lambda a: jnp.concatenate([a[0], a[1]], axis=1).reshape(-1, EXPERT_TM)
    assert idx_p.shape[1] % EXPERT_TM == 0 and idx_s.shape[1] % EXPERT_TM == 0
    y_all = _experts(tiles((idx_p, idx_s)), tiles((gate_p, gate_s)),
                     w_gate[0].astype(bf), w_up[0].astype(bf), w_down[0].astype(bf), xn_all, y_all)

    fg = row(final_norm_g)
    return (_final_norm(y_all, fg, 0, t_p).reshape(x_prompt.shape),
            _final_norm(y_all, fg, t_p, t_s).reshape(x_sample.shape))
```

```python
import functools
import math

import jax
import jax.numpy as jnp
from jax import lax
from jax.experimental import pallas as pl
from jax.experimental.pallas import tpu as pltpu

EPS = 1e-6
CHUNK = 128
GMLP_HEADS = 4
POOL_WINDOWS = (2, 4, 8, 16)
HALO = 8
N_EXPERTS = 16
CAPACITY_FACTOR = 2

LANES = 128
BF16_ROWS = 16
MIXER_TM = 512
EXPERT_TM = 1024
EXPERT_TF = 256
ROUTE_W = 512
VMEM_LIMIT = 56 * 1024 * 1024


def _rms(x, g):
    return x * lax.rsqrt(jnp.mean(x * x, axis=-1, keepdims=True) + EPS) * g


def _gelu_tanh(x):
    c = math.sqrt(2.0 / math.pi)
    return 0.5 * x * (1.0 + jnp.tanh(c * (x + 0.044715 * (x * x * x))))


def _dot(a, b):
    return jnp.dot(a, b, preferred_element_type=jnp.float32)


def _to_token_major(ref, x):
    n, rows = x.shape[0], x.shape[1] // LANES
    for k in range(rows):
        ref[pl.ds(k, n, stride=rows), :] = x[:, k * LANES:(k + 1) * LANES]


def _from_token_major(ref, n):
    rows = ref.shape[0] // n
    for k in range(rows):
        yield ref[pl.ds(k, n, stride=rows), :]


def _mixer_kernel(xa_hbm, xb_hbm, g1_ref, win_ref, vg_ref, ws_ref, bs_ref, wp_ref, bp_ref,
                  ps_ref, wout_ref, g2_ref, wr_ref, x1_ref, xnorm_ref, aff_ref, mix_ref,
                  xbuf, sems, *, tiles_a, tiles_b, seq_a, seq_b):
    tm = x1_ref.shape[0]
    gw = vg_ref.shape[1]
    hd = gw // GMLP_HEADS
    pw = ps_ref.shape[1]
    gd = pw // len(POOL_WINDOWS)
    i = pl.program_id(0)
    in_a = i < tiles_a
    seq_len = jnp.where(in_a, seq_a, seq_b)
    tile = jnp.where(in_a, i, jnp.minimum(i - tiles_a, tiles_b - 1))
    s0 = lax.rem(tile * tm, seq_len)

    def tile_copies(hbm, t, slot):
        n = hbm.shape[0]
        starts = (t * tm, jnp.maximum(t * tm - HALO, 0), jnp.minimum((t + 1) * tm, n - HALO))
        sizes, offs = (tm, HALO, HALO), (0, tm, tm + HALO)
        return [pltpu.make_async_copy(hbm.at[pl.ds(pl.multiple_of(s, HALO), z)],
                                      xbuf.at[slot, pl.ds(o, z)], sems.at[slot, k])
                for k, (s, z, o) in enumerate(zip(starts, sizes, offs))]

    def start_fetch(j, slot):
        @pl.when(j < tiles_a)
        def _():
            for c in tile_copies(xa_hbm, j, slot):
                c.start()

        @pl.when(j >= tiles_a)
        def _():
            for c in tile_copies(xb_hbm, jnp.minimum(j - tiles_a, tiles_b - 1), slot):
                c.start()

    slot = i % 2

    @pl.when(i == 0)
    def _():
        start_fetch(i, slot)

    for c in tile_copies(xa_hbm, 0, slot):
        c.wait()

    @pl.when(i + 1 < pl.num_programs(0))
    def _():
        start_fetch(i + 1, 1 - slot)

    x = xbuf[slot, :tm]
    g1 = g1_ref[...]
    h_ext = _rms(xbuf[slot], g1).astype(jnp.bfloat16)

    z_uv = _gelu_tanh(_dot(h_ext[:tm], win_ref[:, :2 * gw]))
    z_p = _dot(h_ext, win_ref[:, 2 * gw:])

    u = z_uv[:, :gw]
    v = _rms(z_uv[:, gw:], vg_ref[...]).astype(jnp.bfloat16)
    bs = bs_ref[...]
    for c in range(tm // CHUNK):
        rows = slice(c * CHUNK, (c + 1) * CHUNK)
        for hh in range(GMLP_HEADS):
            cols = slice(hh * hd, (hh + 1) * hd)
            sv = _dot(ws_ref[hh], v[rows, cols]) + bs[:, hh:hh + 1]
            mix_ref[rows, cols] = (u[rows, cols] * sv).astype(jnp.bfloat16)

    p = z_p[:tm]
    pe = z_p.astype(jnp.bfloat16)
    ri = lax.broadcasted_iota(jnp.int32, (tm, tm + 2 * HALO), 0) + s0
    q = lax.broadcasted_iota(jnp.int32, (tm, tm + 2 * HALO), 1)
    qpos = jnp.where(q < tm, q, jnp.where(q < tm + HALO, q - tm - HALO, q - HALO)) + s0
    rpos = lax.broadcasted_iota(jnp.int32, (tm, 1), 0) + s0
    for g, w in enumerate(POOL_WINDOWS):
        cols = slice(g * gd, (g + 1) * gd)
        lo = jnp.maximum(ri - w // 2, 0)
        hi = jnp.minimum(ri + w // 2, seq_len)
        band = jnp.where((qpos >= lo) & (qpos < hi), 1.0, 0.0).astype(jnp.bfloat16)
        cnt = jnp.minimum(rpos + w // 2, seq_len) - jnp.maximum(rpos - w // 2, 0)
        d = _dot(band, pe[:, cols]) / cnt.astype(jnp.float32) - p[:, cols]
        yb = _dot(d.astype(jnp.bfloat16), wp_ref[g]) + bp_ref[g:g + 1, :]
        mix_ref[:, gw + g * gd:gw + (g + 1) * gd] = (yb * ps_ref[:, cols]).astype(jnp.bfloat16)

    x1 = x + _dot(mix_ref[...], wout_ref[...])
    x1_ref[...] = x1
    xn = _rms(x1, g2_ref[...])
    xb = xn.astype(jnp.bfloat16)
    words = [pltpu.pack_elementwise([xn[:, 2 * c * LANES:(2 * c + 1) * LANES],
                                     xn[:, (2 * c + 1) * LANES:(2 * c + 2) * LANES]],
                                    packed_dtype=jnp.bfloat16)
             for c in range(xn.shape[1] // (2 * LANES))]
    _to_token_major(xnorm_ref, jnp.concatenate(words, axis=1))
    logits = lax.dot_general(wr_ref[...], xb, (((1,), (1,)), ((), ())),
                             preferred_element_type=jnp.float32)
    e = jnp.exp(logits - jnp.max(logits, axis=0, keepdims=True))
    aff_ref[...] = e / jnp.sum(e, axis=0, keepdims=True)


def _mixer(xa, xb, weights, *, pad_rows):
    d = xa.shape[-1]
    tm = MIXER_TM
    seq_a, seq_b = xa.shape[1], xb.shape[1]
    ta, tb = xa.shape[0] * seq_a, xb.shape[0] * seq_b
    assert seq_a % tm == 0 and seq_b % tm == 0 and tm % CHUNK == 0 and pad_rows % tm == 0
    tiles_a, tiles_b = ta // tm, tb // tm
    n_tiles = tiles_a + tiles_b
    xr = d // (2 * LANES)
    const = lambda a: pl.BlockSpec(a.shape, lambda i: (0,) * a.ndim, pipeline_mode=pl.Buffered(1))
    any_spec = pl.BlockSpec(memory_space=pl.ANY)
    real = lambda i: jnp.minimum(i, n_tiles - 1)
    return pl.pallas_call(
        functools.partial(_mixer_kernel, tiles_a=tiles_a, tiles_b=tiles_b, seq_a=seq_a, seq_b=seq_b),
        grid=(n_tiles + pad_rows // tm,),
        in_specs=[any_spec, any_spec, *[const(w) for w in weights]],
        out_specs=[
            pl.BlockSpec((tm, d), lambda i: (i, 0)),
            pl.BlockSpec((tm * xr, LANES), lambda i: (real(i), 0)),
            pl.BlockSpec((N_EXPERTS, tm), lambda i: (0, real(i))),
        ],
        out_shape=[
            jax.ShapeDtypeStruct((ta + tb + pad_rows, d), jnp.float32),
            jax.ShapeDtypeStruct(((ta + tb) * xr, LANES), jnp.uint32),
            jax.ShapeDtypeStruct((N_EXPERTS, ta + tb), jnp.float32),
        ],
        scratch_shapes=[pltpu.VMEM((tm, weights[8].shape[0]), jnp.bfloat16),
                        pltpu.VMEM((2, tm + 2 * HALO, d), jnp.float32),
                        pltpu.SemaphoreType.DMA((2, 3))],
        compiler_params=pltpu.CompilerParams(
            dimension_semantics=("arbitrary",), vmem_limit_bytes=VMEM_LIMIT),
        name="mixer",
    )(xa.reshape(ta, d), xb.reshape(tb, d), *weights)


def _threshold_kernel(a_ref, thr_ref, *, cap):
    ne = a_ref.shape[0]

    def bisect(i, cur):
        cand = cur | jnp.left_shift(jnp.int32(1), 30 - i)
        cand_f = lax.bitcast_convert_type(cand, jnp.float32)
        n = jnp.sum(jnp.where(a_ref[...] >= cand_f, 1.0, 0.0), axis=1, keepdims=True)
        return jnp.where(n >= cap, cand, cur)
    thr = lax.fori_loop(0, 31, bisect, jnp.zeros((ne, 1), jnp.int32))
    thr_ref[...] = jnp.broadcast_to(lax.bitcast_convert_type(thr, jnp.float32), thr_ref.shape)


def _route_kernel(a_ref, thr_ref, idx_ref, gate_ref, *, cap, token_offset):
    a = a_ref[0]
    nb = a.shape[0]
    thr = thr_ref[0]

    def count(mask):
        s = jnp.sum(jnp.where(mask, 1.0, 0.0), axis=0, keepdims=True)
        return jnp.sum(s, axis=1, keepdims=True)

    jj = lax.broadcasted_iota(jnp.int32, (LANES, LANES), 0)
    kk = lax.broadcasted_iota(jnp.int32, (LANES, LANES), 1)
    upper = jnp.where(jj <= kk, 1.0, 0.0).astype(jnp.bfloat16)
    bi = lax.broadcasted_iota(jnp.int32, (nb, nb), 0)
    bk = lax.broadcasted_iota(jnp.int32, (nb, nb), 1)
    lower = jnp.where(bk <= bi, 1.0, 0.0).astype(jnp.bfloat16)

    def ranks(mask):
        m = jnp.where(mask, 1.0, 0.0).astype(jnp.bfloat16)
        local = _dot(m, upper)
        p_in = jnp.sum(_dot(lower, m), axis=1, keepdims=True)
        p_ex = p_in - local[:, LANES - 1:LANES]
        return local, p_in, p_ex

    gt = a > thr
    eq = a == thr
    need = cap - count(gt)
    eq_local, _, eq_pex = ranks(eq)
    sel = gt | (eq & (eq_pex + eq_local <= need))
    local, p_in, p_ex = ranks(sel)

    a_hi = a.astype(jnp.bfloat16)
    r1 = a - a_hi.astype(jnp.float32)
    a_mid = r1.astype(jnp.bfloat16)
    a_lo = (r1 - a_mid.astype(jnp.float32)).astype(jnp.bfloat16)
    local_t = local.T.astype(jnp.bfloat16)
    hi_t = a_hi.astype(jnp.float32).T.astype(jnp.bfloat16)
    mid_t = a_mid.astype(jnp.float32).T.astype(jnp.bfloat16)
    lo_t = a_lo.astype(jnp.float32).T.astype(jnp.bfloat16)
    jcol = lax.broadcasted_iota(jnp.int32, (LANES, 1), 0).astype(jnp.float32)

    w = idx_ref.shape[2]
    def resolve(ci, carry):
        c = (lax.broadcasted_iota(jnp.int32, (1, w), 1) + ci * w).astype(jnp.float32)
        in_blk = (p_ex <= c) & (c < p_in)
        onehot = jnp.where(in_blk, 1.0, 0.0).astype(jnp.bfloat16)
        blk = jnp.sum(jnp.where(p_in <= c, 1.0, 0.0), axis=0, keepdims=True)
        lc = c - jnp.sum(jnp.where(in_blk, p_ex, 0.0), axis=0, keepdims=True)
        g_local = _dot(local_t, onehot)
        jstar = jnp.sum(jnp.where(g_local <= lc, 1.0, 0.0), axis=0, keepdims=True)
        idx_ref[0, pl.ds(ci, 1), :] = (blk * LANES + jstar).astype(jnp.int32) + token_offset
        g_aff = (_dot(hi_t, onehot) + _dot(mid_t, onehot)) + _dot(lo_t, onehot)
        gate_ref[0, pl.ds(ci, 1), :] = jnp.sum(
            jnp.where(jcol == jstar, g_aff, 0.0), axis=0, keepdims=True)
        return carry
    lax.fori_loop(0, cap // w, resolve, 0)


def _route(aff_t, token_offset):
    ne, t = aff_t.shape
    cap = CAPACITY_FACTOR * t // ne
    nb = t // LANES
    w = ROUTE_W
    assert t % LANES == 0 and nb % 8 == 0 and cap % w == 0
    thr = pl.pallas_call(
        functools.partial(_threshold_kernel, cap=cap),
        out_shape=jax.ShapeDtypeStruct((ne, LANES), jnp.float32),
        name="threshold",
    )(aff_t)
    idx, gate = pl.pallas_call(
        functools.partial(_route_kernel, cap=cap, token_offset=token_offset),
        grid=(ne,),
        in_specs=[pl.BlockSpec((1, nb, LANES), lambda e: (e, 0, 0)),
                  pl.BlockSpec((1, 1, LANES), lambda e: (e, 0, 0))],
        out_specs=[pl.BlockSpec((1, cap // w, w), lambda e: (e, 0, 0)),
                   pl.BlockSpec((1, cap // w, w), lambda e: (e, 0, 0))],
        out_shape=[jax.ShapeDtypeStruct((ne, cap // w, w), jnp.int32),
                   jax.ShapeDtypeStruct((ne, cap // w, w), jnp.float32)],
        compiler_params=pltpu.CompilerParams(dimension_semantics=("arbitrary",)),
        name="route",
    )(aff_t.reshape(ne, nb, LANES), thr.reshape(ne, 1, LANES))
    return idx.reshape(ne, cap), gate.reshape(ne, cap)


def _token_copy(hbm, vmem, tok, slot, rows, sem, to_hbm):
    src, dst = hbm.at[pl.ds(tok * rows, rows)], vmem.at[pl.ds(slot * rows, rows)]
    return pltpu.make_async_copy(dst, src, sem) if to_hbm else pltpu.make_async_copy(src, dst, sem)


DMA_QUEUES = 2


def _start_rows(hbm, idx_ref, vmem, r0, n, rows, sem, *, to_hbm):
    for r in range(n):
        _token_copy(hbm, vmem, idx_ref[0, 0, r0 + r], r0 + r, rows, sem, to_hbm).start(
            priority=r % DMA_QUEUES)


def _start_rows_loop(hbm, idx_ref, vmem, n, rows, sem, *, to_hbm):
    def issue(r, c):
        _token_copy(hbm, vmem, idx_ref[0, 0, r], r, rows, sem, to_hbm).start()
        return c
    lax.fori_loop(0, n, issue, 0, unroll=8)


def _wait_rows(hbm, vmem, sem, *, to_hbm):
    rows = hbm.at[pl.ds(0, vmem.shape[0])]
    (pltpu.make_async_copy(vmem, rows, sem) if to_hbm
     else pltpu.make_async_copy(rows, vmem, sem)).wait()


def _expert_kernel(idx_prev, idx_cur, idx_next, gate_prev, gate_cur, wg_ref, wu_ref, wd_ref,
                   xn_hbm, yin_hbm, y_hbm, xbuf, xbf, pool, gcol, sem_x, sem_y, sem_s):
    del yin_hbm
    ybuf, sbuf, acc = pool.at[0], pool.at[1], pool.at[2]
    i = pl.program_id(0)
    f = pl.program_id(1)
    n_tiles = pl.num_programs(0)
    n_steps = pl.num_programs(1)
    tm = acc.shape[0]
    xr = xbuf.shape[1] // tm
    rows_per_step = tm * wd_ref.shape[0] // wd_ref.shape[1]
    slot = i % 2
    pslot = 1 - slot

    def gated_update(gate_ref):
        gcol[...] = jnp.broadcast_to(gate_ref[0], (LANES, tm)).T
        rb = 64
        def rows_pass(b, carry):
            rows = pl.ds(pl.multiple_of(b * rb, rb), rb)
            gate_col = gcol[rows, :]
            for k in range(acc.shape[1] // LANES):
                cols = slice(k * LANES, (k + 1) * LANES)
                sbuf[rows, cols] = ybuf[rows, cols] + acc[rows, cols] * gate_col
            return carry
        lax.fori_loop(0, tm // rb, rows_pass, 0)

    @pl.when((i == 0) & (f == 0))
    def _():
        sbuf[...] = jnp.zeros(sbuf.shape, sbuf.dtype)
        _start_rows_loop(xn_hbm, idx_cur, xbuf.at[0], tm, xr, sem_x.at[0], to_hbm=False)

    @pl.when(f == 0)
    def _():
        @pl.when(i > 0)
        def _():
            _wait_rows(y_hbm, ybuf, sem_y, to_hbm=False)
            _wait_rows(y_hbm, sbuf, sem_s, to_hbm=True)
            gated_update(gate_prev)

        _wait_rows(xn_hbm, xbuf.at[slot], sem_x.at[slot], to_hbm=False)
        for c, w in enumerate(_from_token_major(xbuf.at[slot], tm)):
            lo, hi = (pltpu.unpack_elementwise(w, index=k, packed_dtype=jnp.bfloat16,
                                               unpacked_dtype=jnp.float32) for k in (0, 1))
            xbf[:, 2 * c * LANES:(2 * c + 1) * LANES] = lo.astype(jnp.bfloat16)
            xbf[:, (2 * c + 1) * LANES:(2 * c + 2) * LANES] = hi.astype(jnp.bfloat16)

    r0 = f * rows_per_step
    copies = (
        functools.partial(_start_rows, xn_hbm, idx_next, xbuf.at[pslot], r0, rows_per_step, xr,
                          sem_x.at[pslot], to_hbm=False),
        functools.partial(_start_rows, y_hbm, idx_cur, ybuf, r0, rows_per_step, 1, sem_y,
                          to_hbm=False),
        functools.partial(_start_rows, y_hbm, idx_prev, sbuf, r0, rows_per_step, 1, sem_s,
                          to_hbm=True),
    )
    blocks = len(copies) + 1
    mb = tm // blocks
    wg, wu, wd = (w[...].astype(jnp.bfloat16) for w in (wg_ref, wu_ref, wd_ref))
    for j in range(blocks):
        rows = slice(j * mb, (j + 1) * mb)
        xb = xbf[rows, :]
        g = _dot(xb, wg)
        u = _dot(xb, wu)
        hcol = (g * (1.0 / (1.0 + jnp.exp(-g))) * u).astype(jnp.bfloat16)
        if j < len(copies):
            copies[j]()
        acc[rows, :] = _dot(hcol, wd) + jnp.where(f > 0, acc[rows, :], 0.0)

    @pl.when((i == n_tiles - 1) & (f == n_steps - 1))
    def _():
        _wait_rows(y_hbm, ybuf, sem_y, to_hbm=False)
        _wait_rows(y_hbm, sbuf, sem_s, to_hbm=True)
        _wait_rows(xn_hbm, xbuf.at[pslot], sem_x.at[pslot], to_hbm=False)
        gated_update(gate_cur)
        _start_rows_loop(y_hbm, idx_cur, sbuf, tm, 1, sem_s, to_hbm=True)
        _wait_rows(y_hbm, sbuf, sem_s, to_hbm=True)


def _experts(idx_tiles, gate_tiles, wg, wu, wd, xn_all, y_all):
    ne, d, dff = wg.shape
    tm, tf = EXPERT_TM, EXPERT_TF
    n_tiles = idx_tiles.shape[0]
    per_expert = n_tiles // ne
    n_steps = dff // tf
    xr = d // (2 * LANES)
    n_tok = xn_all.shape[0] // xr
    assert tm % (4 * n_steps) == 0 and d == dff and y_all.shape[0] >= n_tok + tm
    rows = jnp.arange(tm, dtype=jnp.int32)[None]
    idx_ext = jnp.concatenate([rows + n_tok, idx_tiles, rows], axis=0)
    gate_ext = jnp.concatenate([jnp.zeros((1, tm), jnp.float32), gate_tiles], axis=0)
    idx_ext = idx_ext.reshape(n_tiles + 2, 1, tm)
    gate_ext = gate_ext.reshape(n_tiles + 1, 1, tm)
    any_spec = pl.BlockSpec(memory_space=pl.ANY)
    idx_spec = lambda k: pl.BlockSpec((1, 1, tm), lambda i, f: (i + k, 0, 0), memory_space=pltpu.SMEM)
    gate_spec = lambda k: pl.BlockSpec((1, 1, tm), lambda i, f: (i + k, 0, 0))
    return pl.pallas_call(
        _expert_kernel,
        grid=(n_tiles, n_steps),
        in_specs=[
            idx_spec(0), idx_spec(1), idx_spec(2), gate_spec(0), gate_spec(1),
            pl.BlockSpec((None, d, tf), lambda i, f: (i // per_expert, 0, f)),
            pl.BlockSpec((None, d, tf), lambda i, f: (i // per_expert, 0, f)),
            pl.BlockSpec((None, tf, d), lambda i, f: (i // per_expert, f, 0)),
            any_spec, any_spec,
        ],
        out_specs=any_spec,
        out_shape=jax.ShapeDtypeStruct(y_all.shape, y_all.dtype),
        input_output_aliases={9: 0},
        scratch_shapes=[
            pltpu.VMEM((2, tm * xr, LANES), xn_all.dtype),
            pltpu.VMEM((tm, d), jnp.bfloat16),
            pltpu.VMEM((3, tm, d), jnp.float32),
            pltpu.VMEM((tm, LANES), jnp.float32),
            pltpu.SemaphoreType.DMA((2,)), pltpu.SemaphoreType.DMA, pltpu.SemaphoreType.DMA,
        ],
        compiler_params=pltpu.CompilerParams(
            dimension_semantics=("arbitrary", "arbitrary"), vmem_limit_bytes=VMEM_LIMIT),
        name="experts",
    )(idx_ext, idx_ext, idx_ext, gate_ext, gate_ext, wg, wu, wd, xn_all, y_all)


def _final_kernel(y_ref, g_ref, o_ref):
    o_ref[...] = _rms(y_ref[...], g_ref[...])


def _final_norm(y_all, g, row_offset, t):
    d = g.shape[1]
    tm = 512
    ob = row_offset // tm
    assert row_offset % tm == 0 and t % tm == 0
    return pl.pallas_call(
        _final_kernel,
        grid=(t // tm,),
        in_specs=[pl.BlockSpec((tm, d), lambda i: (i + ob, 0)),
                  pl.BlockSpec((1, d), lambda i: (0, 0))],
        out_specs=pl.BlockSpec((tm, d), lambda i: (i, 0)),
        out_shape=jax.ShapeDtypeStruct((t, d), jnp.float32),
        compiler_params=pltpu.CompilerParams(dimension_semantics=("arbitrary",)),
        name="final_norm",
    )(y_all, g)


def kernel(x_prompt, x_sample, norm1_g, w_in, v_norm_g, w_spatial, b_spatial, w_pool,
           b_pool, pool_scale, w_out, norm2_g, w_router, w_gate, w_up, w_down, final_norm_g):
    assert norm1_g.shape[0] == 1, "single-layer block"
    bf = jnp.bfloat16
    row = lambda a: a.reshape(1, -1)
    mixer_w = (row(norm1_g[0]), w_in[0].astype(bf), row(v_norm_g[0]), w_spatial[0].astype(bf),
               b_spatial[0].T, w_pool[0].astype(bf), b_pool[0], row(pool_scale[0]),
               w_out[0].astype(bf), row(norm2_g[0]), w_router[0].T.astype(bf))

    t_p = x_prompt.shape[0] * x_prompt.shape[1]
    t_s = x_sample.shape[0] * x_sample.shape[1]
    y_all, xn_all, aff = _mixer(x_prompt, x_sample, mixer_w, pad_rows=EXPERT_TM)
    idx_p, gate_p = _route(aff[:, :t_p], 0)
    idx_s, gate_s = _route(aff[:, t_p:], t_p)

    tiles = lambda a: jnp.concatenate([a[0], a[1]], axis=1).reshape(-1, EXPERT_TM)
    assert idx_p.shape[1] % EXPERT_TM == 0 and idx_s.shape[1] % EXPERT_TM == 0
    y_all = _experts(tiles((idx_p, idx_s)), tiles((gate_p, gate_s)),
                     w_gate[0], w_up[0], w_down[0], xn_all, y_all)

    fg = row(final_norm_g)
    return (_final_norm(y_all, fg, 0, t_p).reshape(x_prompt.shape),
            _final_norm(y_all, fg, t_p, t_s).reshape(x_sample.shape))
```

```python
import functools
import math

import jax
import jax.numpy as jnp
from jax import lax
from jax.experimental import pallas as pl
from jax.experimental.pallas import tpu as pltpu

EPS = 1e-6
CHUNK = 128
GMLP_HEADS = 4
POOL_WINDOWS = (2, 4, 8, 16)
HALO = 8
N_EXPERTS = 16
CAPACITY_FACTOR = 2

LANES = 128
BF16_ROWS = 16
MIXER_TM = 512
EXPERT_TM = 1024
EXPERT_TF = 256
ROUTE_W = 512
VMEM_LIMIT = 56 * 1024 * 1024


def _rms(x, g):
    return x * lax.rsqrt(jnp.mean(x * x, axis=-1, keepdims=True) + EPS) * g


def _gelu_tanh(x):
    c = math.sqrt(2.0 / math.pi)
    return 0.5 * x * (1.0 + jnp.tanh(c * (x + 0.044715 * (x * x * x))))


def _dot(a, b):
    return jnp.dot(a, b, preferred_element_type=jnp.float32)


def _to_token_major(ref, x):
    n, rows = x.shape[0], x.shape[1] // LANES
    for k in range(rows):
        ref[pl.ds(k, n, stride=rows), :] = x[:, k * LANES:(k + 1) * LANES]


def _from_token_major(ref, n):
    rows = ref.shape[0] // n
    for k in range(rows):
        yield ref[pl.ds(k, n, stride=rows), :]


def _mixer_kernel(xa_hbm, xb_hbm, g1_ref, win_ref, vg_ref, ws_ref, bs_ref, wp_ref, bp_ref,
                  ps_ref, wout_ref, g2_ref, wr_ref, x1_ref, xnorm_ref, aff_ref, mix_ref,
                  xbuf, sems, *, tiles_a, tiles_b, seq_a, seq_b):
    tm = x1_ref.shape[0]
    gw = vg_ref.shape[1]
    hd = gw // GMLP_HEADS
    pw = ps_ref.shape[1]
    gd = pw // len(POOL_WINDOWS)
    i = pl.program_id(0)
    in_a = i < tiles_a
    seq_len = jnp.where(in_a, seq_a, seq_b)
    tile = jnp.where(in_a, i, jnp.minimum(i - tiles_a, tiles_b - 1))
    s0 = lax.rem(tile * tm, seq_len)

    def tile_copies(hbm, t, slot):
        n = hbm.shape[0]
        starts = (t * tm, jnp.maximum(t * tm - HALO, 0), jnp.minimum((t + 1) * tm, n - HALO))
        sizes, offs = (tm, HALO, HALO), (0, tm, tm + HALO)
        return [pltpu.make_async_copy(hbm.at[pl.ds(pl.multiple_of(s, HALO), z)],
                                      xbuf.at[slot, pl.ds(o, z)], sems.at[slot, k])
                for k, (s, z, o) in enumerate(zip(starts, sizes, offs))]

    def start_fetch(j, slot):
        @pl.when(j < tiles_a)
        def _():
            for c in tile_copies(xa_hbm, j, slot):
                c.start()

        @pl.when(j >= tiles_a)
        def _():
            for c in tile_copies(xb_hbm, jnp.minimum(j - tiles_a, tiles_b - 1), slot):
                c.start()

    slot = i % 2

    @pl.when(i == 0)
    def _():
        start_fetch(i, slot)

    for c in tile_copies(xa_hbm, 0, slot):
        c.wait()

    @pl.when(i + 1 < pl.num_programs(0))
    def _():
        start_fetch(i + 1, 1 - slot)

    x = xbuf[slot, :tm]
    g1 = g1_ref[...]
    h_ext = _rms(xbuf[slot], g1).astype(jnp.bfloat16)

    z_uv = _gelu_tanh(_dot(h_ext[:tm], win_ref[:, :2 * gw]))
    z_p = _dot(h_ext, win_ref[:, 2 * gw:])

    u = z_uv[:, :gw]
    v = _rms(z_uv[:, gw:], vg_ref[...]).astype(jnp.bfloat16)
    bs = bs_ref[...]
    for c in range(tm // CHUNK):
        rows = slice(c * CHUNK, (c + 1) * CHUNK)
        for hh in range(GMLP_HEADS):
            cols = slice(hh * hd, (hh + 1) * hd)
            sv = _dot(ws_ref[hh], v[rows, cols]) + bs[:, hh:hh + 1]
            mix_ref[rows, cols] = (u[rows, cols] * sv).astype(jnp.bfloat16)

    p = z_p[:tm]
    pe = z_p.astype(jnp.bfloat16)
    ri = lax.broadcasted_iota(jnp.int32, (tm, tm + 2 * HALO), 0) + s0
    q = lax.broadcasted_iota(jnp.int32, (tm, tm + 2 * HALO), 1)
    qpos = jnp.where(q < tm, q, jnp.where(q < tm + HALO, q - tm - HALO, q - HALO)) + s0
    rpos = lax.broadcasted_iota(jnp.int32, (tm, 1), 0) + s0
    for g, w in enumerate(POOL_WINDOWS):
        cols = slice(g * gd, (g + 1) * gd)
        lo = jnp.maximum(ri - w // 2, 0)
        hi = jnp.minimum(ri + w // 2, seq_len)
        band = jnp.where((qpos >= lo) & (qpos < hi), 1.0, 0.0).astype(jnp.bfloat16)
        cnt = jnp.minimum(rpos + w // 2, seq_len) - jnp.maximum(rpos - w // 2, 0)
        d = _dot(band, pe[:, cols]) / cnt.astype(jnp.float32) - p[:, cols]
        yb = _dot(d.astype(jnp.bfloat16), wp_ref[g]) + bp_ref[g:g + 1, :]
        mix_ref[:, gw + g * gd:gw + (g + 1) * gd] = (yb * ps_ref[:, cols]).astype(jnp.bfloat16)

    x1 = x + _dot(mix_ref[...], wout_ref[...])
    x1_ref[...] = x1
    xn = _rms(x1, g2_ref[...])
    xb = xn.astype(jnp.bfloat16)
    words = [pltpu.pack_elementwise([xn[:, 2 * c * LANES:(2 * c + 1) * LANES],
                                     xn[:, (2 * c + 1) * LANES:(2 * c + 2) * LANES]],
                                    packed_dtype=jnp.bfloat16)
             for c in range(xn.shape[1] // (2 * LANES))]
    _to_token_major(xnorm_ref, jnp.concatenate(words, axis=1))
    logits = lax.dot_general(wr_ref[...], xb, (((1,), (1,)), ((), ())),
                             preferred_element_type=jnp.float32)
    e = jnp.exp(logits - jnp.max(logits, axis=0, keepdims=True))
    aff_ref[...] = e / jnp.sum(e, axis=0, keepdims=True)


def _mixer(xa, xb, weights, *, pad_rows):
    d = xa.shape[-1]
    tm = MIXER_TM
    seq_a, seq_b = xa.shape[1], xb.shape[1]
    ta, tb = xa.shape[0] * seq_a, xb.shape[0] * seq_b
    assert seq_a % tm == 0 and seq_b % tm == 0 and tm % CHUNK == 0 and pad_rows % tm == 0
    tiles_a, tiles_b = ta // tm, tb // tm
    n_tiles = tiles_a + tiles_b
    xr = d // (2 * LANES)
    const = lambda a: pl.BlockSpec(a.shape, lambda i: (0,) * a.ndim, pipeline_mode=pl.Buffered(1))
    any_spec = pl.BlockSpec(memory_space=pl.ANY)
    real = lambda i: jnp.minimum(i, n_tiles - 1)
    return pl.pallas_call(
        functools.partial(_mixer_kernel, tiles_a=tiles_a, tiles_b=tiles_b, seq_a=seq_a, seq_b=seq_b),
        grid=(n_tiles + pad_rows // tm,),
        in_specs=[any_spec, any_spec, *[const(w) for w in weights]],
        out_specs=[
            pl.BlockSpec((tm, d), lambda i: (i, 0)),
            pl.BlockSpec((tm * xr, LANES), lambda i: (real(i), 0)),
            pl.BlockSpec((N_EXPERTS, tm), lambda i: (0, real(i))),
        ],
        out_shape=[
            jax.ShapeDtypeStruct((ta + tb + pad_rows, d), jnp.float32),
            jax.ShapeDtypeStruct(((ta + tb) * xr, LANES), jnp.uint32),
            jax.ShapeDtypeStruct((N_EXPERTS, ta + tb), jnp.float32),
        ],
        scratch_shapes=[pltpu.VMEM((tm, weights[8].shape[0]), jnp.bfloat16),
                        pltpu.VMEM((2, tm + 2 * HALO, d), jnp.float32),
                        pltpu.SemaphoreType.DMA((2, 3))],
        compiler_params=pltpu.CompilerParams(
            dimension_semantics=("arbitrary",), vmem_limit_bytes=VMEM_LIMIT),
        name="mixer",
    )(xa.reshape(ta, d), xb.reshape(tb, d), *weights)


def _threshold_kernel(a_ref, thr_ref, *, cap):
    ne = a_ref.shape[0]

    def bisect(i, cur):
        cand = cur | jnp.left_shift(jnp.int32(1), 30 - i)
        cand_f = lax.bitcast_convert_type(cand, jnp.float32)
        n = jnp.sum(jnp.where(a_ref[...] >= cand_f, 1.0, 0.0), axis=1, keepdims=True)
        return jnp.where(n >= cap, cand, cur)
    thr = lax.fori_loop(0, 31, bisect, jnp.zeros((ne, 1), jnp.int32))
    thr_ref[...] = jnp.broadcast_to(lax.bitcast_convert_type(thr, jnp.float32), thr_ref.shape)


def _route_kernel(a_ref, thr_ref, idx_ref, gate_ref, *, cap, token_offset):
    a = a_ref[0]
    nb = a.shape[0]
    thr = thr_ref[0]

    def count(mask):
        s = jnp.sum(jnp.where(mask, 1.0, 0.0), axis=0, keepdims=True)
        return jnp.sum(s, axis=1, keepdims=True)

    jj = lax.broadcasted_iota(jnp.int32, (LANES, LANES), 0)
    kk = lax.broadcasted_iota(jnp.int32, (LANES, LANES), 1)
    upper = jnp.where(jj <= kk, 1.0, 0.0).astype(jnp.bfloat16)
    bi = lax.broadcasted_iota(jnp.int32, (nb, nb), 0)
    bk = lax.broadcasted_iota(jnp.int32, (nb, nb), 1)
    lower = jnp.where(bk <= bi, 1.0, 0.0).astype(jnp.bfloat16)

    def ranks(mask):
        m = jnp.where(mask, 1.0, 0.0).astype(jnp.bfloat16)
        local = _dot(m, upper)
        p_in = jnp.sum(_dot(lower, m), axis=1, keepdims=True)
        p_ex = p_in - local[:, LANES - 1:LANES]
        return local, p_in, p_ex

    gt = a > thr
    eq = a == thr
    need = cap - count(gt)
    eq_local, _, eq_pex = ranks(eq)
    sel = gt | (eq & (eq_pex + eq_local <= need))
    local, p_in, p_ex = ranks(sel)

    a_hi = a.astype(jnp.bfloat16)
    r1 = a - a_hi.astype(jnp.float32)
    a_mid = r1.astype(jnp.bfloat16)
    a_lo = (r1 - a_mid.astype(jnp.float32)).astype(jnp.bfloat16)
    local_t = local.T.astype(jnp.bfloat16)
    hi_t = a_hi.astype(jnp.float32).T.astype(jnp.bfloat16)
    mid_t = a_mid.astype(jnp.float32).T.astype(jnp.bfloat16)
    lo_t = a_lo.astype(jnp.float32).T.astype(jnp.bfloat16)
    jcol = lax.broadcasted_iota(jnp.int32, (LANES, 1), 0).astype(jnp.float32)

    w = idx_ref.shape[2]
    def resolve(ci, carry):
        c = (lax.broadcasted_iota(jnp.int32, (1, w), 1) + ci * w).astype(jnp.float32)
        in_blk = (p_ex <= c) & (c < p_in)
        onehot = jnp.where(in_blk, 1.0, 0.0).astype(jnp.bfloat16)
        blk = jnp.sum(jnp.where(p_in <= c, 1.0, 0.0), axis=0, keepdims=True)
        lc = c - jnp.sum(jnp.where(in_blk, p_ex, 0.0), axis=0, keepdims=True)
        g_local = _dot(local_t, onehot)
        jstar = jnp.sum(jnp.where(g_local <= lc, 1.0, 0.0), axis=0, keepdims=True)
        idx_ref[0, pl.ds(ci, 1), :] = (blk * LANES + jstar).astype(jnp.int32) + token_offset
        g_aff = (_dot(hi_t, onehot) + _dot(mid_t, onehot)) + _dot(lo_t, onehot)
        gate_ref[0, pl.ds(ci, 1), :] = jnp.sum(
            jnp.where(jcol == jstar, g_aff, 0.0), axis=0, keepdims=True)
        return carry
    lax.fori_loop(0, cap // w, resolve, 0)


def _route(aff_t, token_offset):
    ne, t = aff_t.shape
    cap = CAPACITY_FACTOR * t // ne
    nb = t // LANES
    w = ROUTE_W
    assert t % LANES == 0 and nb % 8 == 0 and cap % w == 0
    thr = pl.pallas_call(
        functools.partial(_threshold_kernel, cap=cap),
        out_shape=jax.ShapeDtypeStruct((ne, LANES), jnp.float32),
        name="threshold",
    )(aff_t)
    idx, gate = pl.pallas_call(
        functools.partial(_route_kernel, cap=cap, token_offset=token_offset),
        grid=(ne,),
        in_specs=[pl.BlockSpec((1, nb, LANES), lambda e: (e, 0, 0)),
                  pl.BlockSpec((1, 1, LANES), lambda e: (e, 0, 0))],
        out_specs=[pl.BlockSpec((1, cap // w, w), lambda e: (e, 0, 0)),
                   pl.BlockSpec((1, cap // w, w), lambda e: (e, 0, 0))],
        out_shape=[jax.ShapeDtypeStruct((ne, cap // w, w), jnp.int32),
                   jax.ShapeDtypeStruct((ne, cap // w, w), jnp.float32)],
        compiler_params=pltpu.CompilerParams(dimension_semantics=("arbitrary",)),
        name="route",
    )(aff_t.reshape(ne, nb, LANES), thr.reshape(ne, 1, LANES))
    return idx.reshape(ne, cap), gate.reshape(ne, cap)


def _token_copy(hbm, vmem, tok, slot, rows, sem, to_hbm):
    src, dst = hbm.at[pl.ds(tok * rows, rows)], vmem.at[pl.ds(slot * rows, rows)]
    return pltpu.make_async_copy(dst, src, sem) if to_hbm else pltpu.make_async_copy(src, dst, sem)


DMA_QUEUES = 2


def _start_rows(hbm, idx_ref, vmem, r0, n, rows, sem, *, to_hbm):
    for r in range(n):
        _token_copy(hbm, vmem, idx_ref[0, 0, r0 + r], r0 + r, rows, sem, to_hbm).start(
            priority=r % DMA_QUEUES)


def _start_rows_loop(hbm, idx_ref, vmem, n, rows, sem, *, to_hbm):
    def issue(r, c):
        _token_copy(hbm, vmem, idx_ref[0, 0, r], r, rows, sem, to_hbm).start()
        return c
    lax.fori_loop(0, n, issue, 0, unroll=8)


def _wait_rows(hbm, vmem, sem, *, to_hbm):
    rows = hbm.at[pl.ds(0, vmem.shape[0])]
    (pltpu.make_async_copy(vmem, rows, sem) if to_hbm
     else pltpu.make_async_copy(rows, vmem, sem)).wait()


def _expert_kernel(idx_prev, idx_cur, idx_next, gate_prev, gate_cur, wg_ref, wu_ref, wd_ref,
                   xn_hbm, yin_hbm, y_hbm, xbuf, xbf, pool, gcol, sem_x, sem_y, sem_s):
    del yin_hbm
    ybuf, sbuf, acc = pool.at[0], pool.at[1], pool.at[2]
    i = pl.program_id(0)
    f = pl.program_id(1)
    n_tiles = pl.num_programs(0)
    n_steps = pl.num_programs(1)
    tm = acc.shape[0]
    xr = xbuf.shape[1] // tm
    rows_per_step = tm * wd_ref.shape[0] // wd_ref.shape[1]
    slot = i % 2
    pslot = 1 - slot

    def gated_update(gate_ref):
        gcol[...] = jnp.broadcast_to(gate_ref[0], (LANES, tm)).T
        rb = 64
        def rows_pass(b, carry):
            rows = pl.ds(pl.multiple_of(b * rb, rb), rb)
            gate_col = gcol[rows, :]
            for k in range(acc.shape[1] // LANES):
                cols = slice(k * LANES, (k + 1) * LANES)
                sbuf[rows, cols] = ybuf[rows, cols] + acc[rows, cols] * gate_col
            return carry
        lax.fori_loop(0, tm // rb, rows_pass, 0)

    @pl.when((i == 0) & (f == 0))
    def _():
        sbuf[...] = jnp.zeros(sbuf.shape, sbuf.dtype)
        _start_rows_loop(xn_hbm, idx_cur, xbuf.at[0], tm, xr, sem_x.at[0], to_hbm=False)

    @pl.when(f == 0)
    def _():
        @pl.when(i > 0)
        def _():
            _wait_rows(y_hbm, ybuf, sem_y, to_hbm=False)
            _wait_rows(y_hbm, sbuf, sem_s, to_hbm=True)
            gated_update(gate_prev)

        _wait_rows(xn_hbm, xbuf.at[slot], sem_x.at[slot], to_hbm=False)
        for c, w in enumerate(_from_token_major(xbuf.at[slot], tm)):
            lo, hi = (pltpu.unpack_elementwise(w, index=k, packed_dtype=jnp.bfloat16,
                                               unpacked_dtype=jnp.float32) for k in (0, 1))
            xbf[:, 2 * c * LANES:(2 * c + 1) * LANES] = lo.astype(jnp.bfloat16)
            xbf[:, (2 * c + 1) * LANES:(2 * c + 2) * LANES] = hi.astype(jnp.bfloat16)

    r0 = f * rows_per_step
    copies = (
        functools.partial(_start_rows, xn_hbm, idx_next, xbuf.at[pslot], r0, rows_per_step, xr,
                          sem_x.at[pslot], to_hbm=False),
        functools.partial(_start_rows, y_hbm, idx_cur, ybuf, r0, rows_per_step, 1, sem_y,
                          to_hbm=False),
        functools.partial(_start_rows, y_hbm, idx_prev, sbuf, r0, rows_per_step, 1, sem_s,
                          to_hbm=True),
    )
    blocks = len(copies) + 1
    mb = tm // blocks
    wg, wu, wd = (w[...].astype(jnp.bfloat16) for w in (wg_ref, wu_ref, wd_ref))
    def activation(j):
        xb = xbf[j * mb:(j + 1) * mb, :]
        g = _dot(xb, wg)
        u = _dot(xb, wu)
        return (g * (1.0 / (1.0 + jnp.exp(-g))) * u).astype(jnp.bfloat16)

    hcol = activation(0)
    for j in range(blocks):
        rows = slice(j * mb, (j + 1) * mb)
        hnext = activation(j + 1) if j + 1 < blocks else None
        if j < len(copies):
            copies[j]()
        acc[rows, :] = _dot(hcol, wd) + jnp.where(f > 0, acc[rows, :], 0.0)
        hcol = hnext

    @pl.when((i == n_tiles - 1) & (f == n_steps - 1))
    def _():
        _wait_rows(y_hbm, ybuf, sem_y, to_hbm=False)
        _wait_rows(y_hbm, sbuf, sem_s, to_hbm=True)
        _wait_rows(xn_hbm, xbuf.at[pslot], sem_x.at[pslot], to_hbm=False)
        gated_update(gate_cur)
        _start_rows_loop(y_hbm, idx_cur, sbuf, tm, 1, sem_s, to_hbm=True)
        _wait_rows(y_hbm, sbuf, sem_s, to_hbm=True)


def _experts(idx_tiles, gate_tiles, wg, wu, wd, xn_all, y_all):
    ne, d, dff = wg.shape
    tm, tf = EXPERT_TM, EXPERT_TF
    n_tiles = idx_tiles.shape[0]
    per_expert = n_tiles // ne
    n_steps = dff // tf
    xr = d // (2 * LANES)
    n_tok = xn_all.shape[0] // xr
    assert tm % (4 * n_steps) == 0 and d == dff and y_all.shape[0] >= n_tok + tm
    rows = jnp.arange(tm, dtype=jnp.int32)[None]
    idx_ext = jnp.concatenate([rows + n_tok, idx_tiles, rows], axis=0)
    gate_ext = jnp.concatenate([jnp.zeros((1, tm), jnp.float32), gate_tiles], axis=0)
    idx_ext = idx_ext.reshape(n_tiles + 2, 1, tm)
    gate_ext = gate_ext.reshape(n_tiles + 1, 1, tm)
    any_spec = pl.BlockSpec(memory_space=pl.ANY)
    idx_spec = lambda k: pl.BlockSpec((1, 1, tm), lambda i, f: (i + k, 0, 0), memory_space=pltpu.SMEM)
    gate_spec = lambda k: pl.BlockSpec((1, 1, tm), lambda i, f: (i + k, 0, 0))
    return pl.pallas_call(
        _expert_kernel,
        grid=(n_tiles, n_steps),
        in_specs=[
            idx_spec(0), idx_spec(1), idx_spec(2), gate_spec(0), gate_spec(1),
            pl.BlockSpec((None, d, tf), lambda i, f: (i // per_expert, 0, f)),
            pl.BlockSpec((None, d, tf), lambda i, f: (i // per_expert, 0, f)),
            pl.BlockSpec((None, tf, d), lambda i, f: (i // per_expert, f, 0)),
            any_spec, any_spec,
        ],
        out_specs=any_spec,
        out_shape=jax.ShapeDtypeStruct(y_all.shape, y_all.dtype),
        input_output_aliases={9: 0},
        scratch_shapes=[
            pltpu.VMEM((2, tm * xr, LANES), xn_all.dtype),
            pltpu.VMEM((tm, d), jnp.bfloat16),
            pltpu.VMEM((3, tm, d), jnp.float32),
            pltpu.VMEM((tm, LANES), jnp.float32),
            pltpu.SemaphoreType.DMA((2,)), pltpu.SemaphoreType.DMA, pltpu.SemaphoreType.DMA,
        ],
        compiler_params=pltpu.CompilerParams(
            dimension_semantics=("arbitrary", "arbitrary"), vmem_limit_bytes=VMEM_LIMIT),
        name="experts",
    )(idx_ext, idx_ext, idx_ext, gate_ext, gate_ext, wg, wu, wd, xn_all, y_all)


def _final_kernel(y_ref, g_ref, o_ref):
    o_ref[...] = _rms(y_ref[...], g_ref[...])


def _final_norm(y_all, g, row_offset, t):
    d = g.shape[1]
    tm = 512
    ob = row_offset // tm
    assert row_offset % tm == 0 and t % tm == 0
    return pl.pallas_call(
        _final_kernel,
        grid=(t // tm,),
        in_specs=[pl.BlockSpec((tm, d), lambda i: (i + ob, 0)),
                  pl.BlockSpec((1, d), lambda i: (0, 0))],
        out_specs=pl.BlockSpec((tm, d), lambda i: (i, 0)),
        out_shape=jax.ShapeDtypeStruct((t, d), jnp.float32),
        compiler_params=pltpu.CompilerParams(dimension_semantics=("arbitrary",)),
        name="final_norm",
    )(y_all, g)


def kernel(x_prompt, x_sample, norm1_g, w_in, v_norm_g, w_spatial, b_spatial, w_pool,
           b_pool, pool_scale, w_out, norm2_g, w_router, w_gate, w_up, w_down, final_norm_g):
    assert norm1_g.shape[0] == 1, "single-layer block"
    bf = jnp.bfloat16
    row = lambda a: a.reshape(1, -1)
    mixer_w = (row(norm1_g[0]), w_in[0].astype(bf), row(v_norm_g[0]), w_spatial[0].astype(bf),
               b_spatial[0].T, w_pool[0].astype(bf), b_pool[0], row(pool_scale[0]),
               w_out[0].astype(bf), row(norm2_g[0]), w_router[0].T.astype(bf))

    t_p = x_prompt.shape[0] * x_prompt.shape[1]
    t_s = x_sample.shape[0] * x_sample.shape[1]
    y_all, xn_all, aff = _mixer(x_prompt, x_sample, mixer_w, pad_rows=EXPERT_TM)
    idx_p, gate_p = _route(aff[:, :t_p], 0)
    idx_s, gate_s = _route(aff[:, t_p:], t_p)

    tiles = lambda a: jnp.concatenate([a[0], a[1]], axis=1).reshape(-1, EXPERT_TM)
    assert idx_p.shape[1] % EXPERT_TM == 0 and idx_s.shape[1] % EXPERT_TM == 0
    y_all = _experts(tiles((idx_p, idx_s)), tiles((gate_p, gate_s)),
                     w_gate[0], w_up[0], w_down[0], xn_all, y_all)

    fg = row(final_norm_g)
    return (_final_norm(y_all, fg, 0, t_p).reshape(x_prompt.shape),
            _final_norm(y_all, fg, t_p, t_s).reshape(x_sample.shape))
```

```python
import functools
import math

import jax
import jax.numpy as jnp
from jax import lax
from jax.experimental import pallas as pl
from jax.experimental.pallas import tpu as pltpu

EPS = 1e-6
CHUNK = 128
GMLP_HEADS = 4
POOL_WINDOWS = (2, 4, 8, 16)
HALO = 8
N_EXPERTS = 16
CAPACITY_FACTOR = 2

LANES = 128
BF16_ROWS = 16
MIXER_TM = 512
EXPERT_TM = 1024
EXPERT_TF = 256
ROUTE_W = 512
VMEM_LIMIT = 56 * 1024 * 1024


def _rms(x, g):
    return x * lax.rsqrt(jnp.mean(x * x, axis=-1, keepdims=True) + EPS) * g


def _gelu_tanh(x):
    c = math.sqrt(2.0 / math.pi)
    return 0.5 * x * (1.0 + jnp.tanh(c * (x + 0.044715 * (x * x * x))))


def _dot(a, b):
    return jnp.dot(a, b, preferred_element_type=jnp.float32)


def _to_token_major(ref, x):
    n, rows = x.shape[0], x.shape[1] // LANES
    for k in range(rows):
        ref[pl.ds(k, n, stride=rows), :] = x[:, k * LANES:(k + 1) * LANES]


def _from_token_major(ref, n):
    rows = ref.shape[0] // n
    for k in range(rows):
        yield ref[pl.ds(k, n, stride=rows), :]


def _mixer_kernel(xa_hbm, xb_hbm, g1_ref, win_ref, vg_ref, ws_ref, bs_ref, wp_ref, bp_ref,
                  ps_ref, wout_ref, g2_ref, wr_ref, x1_ref, xnorm_ref, aff_ref, mix_ref,
                  xbuf, sems, *, tiles_a, tiles_b, seq_a, seq_b):
    tm = x1_ref.shape[0]
    gw = vg_ref.shape[1]
    hd = gw // GMLP_HEADS
    pw = ps_ref.shape[1]
    gd = pw // len(POOL_WINDOWS)
    i = pl.program_id(0)
    in_a = i < tiles_a
    seq_len = jnp.where(in_a, seq_a, seq_b)
    tile = jnp.where(in_a, i, jnp.minimum(i - tiles_a, tiles_b - 1))
    s0 = lax.rem(tile * tm, seq_len)

    def tile_copies(hbm, t, slot):
        n = hbm.shape[0]
        starts = (t * tm, jnp.maximum(t * tm - HALO, 0), jnp.minimum((t + 1) * tm, n - HALO))
        sizes, offs = (tm, HALO, HALO), (0, tm, tm + HALO)
        return [pltpu.make_async_copy(hbm.at[pl.ds(pl.multiple_of(s, HALO), z)],
                                      xbuf.at[slot, pl.ds(o, z)], sems.at[slot, k])
                for k, (s, z, o) in enumerate(zip(starts, sizes, offs))]

    def start_fetch(j, slot):
        @pl.when(j < tiles_a)
        def _():
            for c in tile_copies(xa_hbm, j, slot):
                c.start()

        @pl.when(j >= tiles_a)
        def _():
            for c in tile_copies(xb_hbm, jnp.minimum(j - tiles_a, tiles_b - 1), slot):
                c.start()

    slot = i % 2

    @pl.when(i == 0)
    def _():
        start_fetch(i, slot)

    for c in tile_copies(xa_hbm, 0, slot):
        c.wait()

    @pl.when(i + 1 < pl.num_programs(0))
    def _():
        start_fetch(i + 1, 1 - slot)

    x = xbuf[slot, :tm]
    g1 = g1_ref[...]
    h_ext = _rms(xbuf[slot], g1).astype(jnp.bfloat16)

    z_uv = _gelu_tanh(_dot(h_ext[:tm], win_ref[:, :2 * gw]))
    z_p = _dot(h_ext, win_ref[:, 2 * gw:])

    u = z_uv[:, :gw]
    v = _rms(z_uv[:, gw:], vg_ref[...]).astype(jnp.bfloat16)
    bs = bs_ref[...]
    for c in range(tm // CHUNK):
        rows = slice(c * CHUNK, (c + 1) * CHUNK)
        for hh in range(GMLP_HEADS):
            cols = slice(hh * hd, (hh + 1) * hd)
            sv = _dot(ws_ref[hh], v[rows, cols]) + bs[:, hh:hh + 1]
            mix_ref[rows, cols] = (u[rows, cols] * sv).astype(jnp.bfloat16)

    p = z_p[:tm]
    n_ext = tm + 2 * HALO
    e_all = jnp.concatenate([jnp.where(s0 > 0, z_p[tm:tm + HALO], 0.0), p,
                             jnp.where(s0 + tm < seq_len, z_p[tm + HALO:], 0.0)], axis=0)
    shift_up = lambda a, k: pltpu.roll(a, n_ext - k, 0)
    rpos = lax.broadcasted_iota(jnp.int32, (tm, 1), 0) + s0
    for g, w in enumerate(POOL_WINDOWS):
        cols = slice(g * gd, (g + 1) * gd)
        run, k = e_all[:, cols], 1
        while k < w:
            run, k = run + shift_up(run, k), 2 * k
        win = (shift_up(run, HALO - w // 2) if w // 2 < HALO else run)[:tm]
        cnt = jnp.minimum(rpos + w // 2, seq_len) - jnp.maximum(rpos - w // 2, 0)
        d = win / cnt.astype(jnp.float32) - p[:, cols]
        yb = _dot(d.astype(jnp.bfloat16), wp_ref[g]) + bp_ref[g:g + 1, :]
        mix_ref[:, gw + g * gd:gw + (g + 1) * gd] = (yb * ps_ref[:, cols]).astype(jnp.bfloat16)

    x1 = x + _dot(mix_ref[...], wout_ref[...])
    x1_ref[...] = x1
    xn = _rms(x1, g2_ref[...])
    xb = xn.astype(jnp.bfloat16)
    words = [pltpu.pack_elementwise([xn[:, 2 * c * LANES:(2 * c + 1) * LANES],
                                     xn[:, (2 * c + 1) * LANES:(2 * c + 2) * LANES]],
                                    packed_dtype=jnp.bfloat16)
             for c in range(xn.shape[1] // (2 * LANES))]
    _to_token_major(xnorm_ref, jnp.concatenate(words, axis=1))
    logits = lax.dot_general(wr_ref[...], xb, (((1,), (1,)), ((), ())),
                             preferred_element_type=jnp.float32)
    e = jnp.exp(logits - jnp.max(logits, axis=0, keepdims=True))
    aff_ref[...] = e / jnp.sum(e, axis=0, keepdims=True)


def _mixer(xa, xb, weights, *, pad_rows):
    d = xa.shape[-1]
    tm = MIXER_TM
    seq_a, seq_b = xa.shape[1], xb.shape[1]
    ta, tb = xa.shape[0] * seq_a, xb.shape[0] * seq_b
    assert seq_a % tm == 0 and seq_b % tm == 0 and tm % CHUNK == 0 and pad_rows % tm == 0
    tiles_a, tiles_b = ta // tm, tb // tm
    n_tiles = tiles_a + tiles_b
    xr = d // (2 * LANES)
    const = lambda a: pl.BlockSpec(a.shape, lambda i: (0,) * a.ndim, pipeline_mode=pl.Buffered(1))
    any_spec = pl.BlockSpec(memory_space=pl.ANY)
    real = lambda i: jnp.minimum(i, n_tiles - 1)
    return pl.pallas_call(
        functools.partial(_mixer_kernel, tiles_a=tiles_a, tiles_b=tiles_b, seq_a=seq_a, seq_b=seq_b),
        grid=(n_tiles + pad_rows // tm,),
        in_specs=[any_spec, any_spec, *[const(w) for w in weights]],
        out_specs=[
            pl.BlockSpec((tm, d), lambda i: (i, 0)),
            pl.BlockSpec((tm * xr, LANES), lambda i: (real(i), 0)),
            pl.BlockSpec((N_EXPERTS, tm), lambda i: (0, real(i))),
        ],
        out_shape=[
            jax.ShapeDtypeStruct((ta + tb + pad_rows, d), jnp.float32),
            jax.ShapeDtypeStruct(((ta + tb) * xr, LANES), jnp.uint32),
            jax.ShapeDtypeStruct((N_EXPERTS, ta + tb), jnp.float32),
        ],
        scratch_shapes=[pltpu.VMEM((tm, weights[8].shape[0]), jnp.bfloat16),
                        pltpu.VMEM((2, tm + 2 * HALO, d), jnp.float32),
                        pltpu.SemaphoreType.DMA((2, 3))],
        compiler_params=pltpu.CompilerParams(
            dimension_semantics=("arbitrary",), vmem_limit_bytes=VMEM_LIMIT),
        name="mixer",
    )(xa.reshape(ta, d), xb.reshape(tb, d), *weights)


def _threshold_kernel(a_ref, thr_ref, *, cap):
    ne = a_ref.shape[0]

    def bisect(i, cur):
        cand = cur | jnp.left_shift(jnp.int32(1), 30 - i)
        cand_f = lax.bitcast_convert_type(cand, jnp.float32)
        n = jnp.sum(jnp.where(a_ref[...] >= cand_f, 1.0, 0.0), axis=1, keepdims=True)
        return jnp.where(n >= cap, cand, cur)
    thr = lax.fori_loop(0, 31, bisect, jnp.zeros((ne, 1), jnp.int32))
    thr_ref[...] = jnp.broadcast_to(lax.bitcast_convert_type(thr, jnp.float32), thr_ref.shape)


def _route_kernel(a_ref, thr_ref, idx_ref, gate_ref, *, cap, token_offset):
    a = a_ref[0]
    nb = a.shape[0]
    thr = thr_ref[0]

    def count(mask):
        s = jnp.sum(jnp.where(mask, 1.0, 0.0), axis=0, keepdims=True)
        return jnp.sum(s, axis=1, keepdims=True)

    jj = lax.broadcasted_iota(jnp.int32, (LANES, LANES), 0)
    kk = lax.broadcasted_iota(jnp.int32, (LANES, LANES), 1)
    upper = jnp.where(jj <= kk, 1.0, 0.0).astype(jnp.bfloat16)
    bi = lax.broadcasted_iota(jnp.int32, (nb, nb), 0)
    bk = lax.broadcasted_iota(jnp.int32, (nb, nb), 1)
    lower = jnp.where(bk <= bi, 1.0, 0.0).astype(jnp.bfloat16)

    def ranks(mask):
        m = jnp.where(mask, 1.0, 0.0).astype(jnp.bfloat16)
        local = _dot(m, upper)
        p_in = jnp.sum(_dot(lower, m), axis=1, keepdims=True)
        p_ex = p_in - local[:, LANES - 1:LANES]
        return local, p_in, p_ex

    gt = a > thr
    eq = a == thr
    need = cap - count(gt)
    eq_local, _, eq_pex = ranks(eq)
    sel = gt | (eq & (eq_pex + eq_local <= need))
    local, p_in, p_ex = ranks(sel)

    a_hi = a.astype(jnp.bfloat16)
    r1 = a - a_hi.astype(jnp.float32)
    a_mid = r1.astype(jnp.bfloat16)
    a_lo = (r1 - a_mid.astype(jnp.float32)).astype(jnp.bfloat16)
    local_t = local.T.astype(jnp.bfloat16)
    hi_t = a_hi.astype(jnp.float32).T.astype(jnp.bfloat16)
    mid_t = a_mid.astype(jnp.float32).T.astype(jnp.bfloat16)
    lo_t = a_lo.astype(jnp.float32).T.astype(jnp.bfloat16)
    jcol = lax.broadcasted_iota(jnp.int32, (LANES, 1), 0).astype(jnp.float32)

    w = idx_ref.shape[2]
    def resolve(ci, carry):
        c = (lax.broadcasted_iota(jnp.int32, (1, w), 1) + ci * w).astype(jnp.float32)
        in_blk = (p_ex <= c) & (c < p_in)
        onehot = jnp.where(in_blk, 1.0, 0.0).astype(jnp.bfloat16)
        blk = jnp.sum(jnp.where(p_in <= c, 1.0, 0.0), axis=0, keepdims=True)
        lc = c - jnp.sum(jnp.where(in_blk, p_ex, 0.0), axis=0, keepdims=True)
        g_local = _dot(local_t, onehot)
        jstar = jnp.sum(jnp.where(g_local <= lc, 1.0, 0.0), axis=0, keepdims=True)
        idx_ref[0, pl.ds(ci, 1), :] = (blk * LANES + jstar).astype(jnp.int32) + token_offset
        g_aff = (_dot(hi_t, onehot) + _dot(mid_t, onehot)) + _dot(lo_t, onehot)
        gate_ref[0, pl.ds(ci, 1), :] = jnp.sum(
            jnp.where(jcol == jstar, g_aff, 0.0), axis=0, keepdims=True)
        return carry
    lax.fori_loop(0, cap // w, resolve, 0)


def _route(aff_t, token_offset):
    ne, t = aff_t.shape
    cap = CAPACITY_FACTOR * t // ne
    nb = t // LANES
    w = ROUTE_W
    assert t % LANES == 0 and nb % 8 == 0 and cap % w == 0
    thr = pl.pallas_call(
        functools.partial(_threshold_kernel, cap=cap),
        out_shape=jax.ShapeDtypeStruct((ne, LANES), jnp.float32),
        name="threshold",
    )(aff_t)
    idx, gate = pl.pallas_call(
        functools.partial(_route_kernel, cap=cap, token_offset=token_offset),
        grid=(ne,),
        in_specs=[pl.BlockSpec((1, nb, LANES), lambda e: (e, 0, 0)),
                  pl.BlockSpec((1, 1, LANES), lambda e: (e, 0, 0))],
        out_specs=[pl.BlockSpec((1, cap // w, w), lambda e: (e, 0, 0)),
                   pl.BlockSpec((1, cap // w, w), lambda e: (e, 0, 0))],
        out_shape=[jax.ShapeDtypeStruct((ne, cap // w, w), jnp.int32),
                   jax.ShapeDtypeStruct((ne, cap // w, w), jnp.float32)],
        compiler_params=pltpu.CompilerParams(dimension_semantics=("arbitrary",)),
        name="route",
    )(aff_t.reshape(ne, nb, LANES), thr.reshape(ne, 1, LANES))
    return idx.reshape(ne, cap), gate.reshape(ne, cap)


def _token_copy(hbm, vmem, tok, slot, rows, sem, to_hbm):
    src, dst = hbm.at[pl.ds(tok * rows, rows)], vmem.at[pl.ds(slot * rows, rows)]
    return pltpu.make_async_copy(dst, src, sem) if to_hbm else pltpu.make_async_copy(src, dst, sem)


DMA_QUEUES = 2


def _start_rows(hbm, idx_ref, vmem, r0, n, rows, sem, *, to_hbm):
    for r in range(n):
        _token_copy(hbm, vmem, idx_ref[0, 0, r0 + r], r0 + r, rows, sem, to_hbm).start(
            priority=r % DMA_QUEUES)


def _start_rows_loop(hbm, idx_ref, vmem, n, rows, sem, *, to_hbm):
    def issue(r, c):
        _token_copy(hbm, vmem, idx_ref[0, 0, r], r, rows, sem, to_hbm).start()
        return c
    lax.fori_loop(0, n, issue, 0, unroll=8)


def _wait_rows(hbm, vmem, sem, *, to_hbm):
    rows = hbm.at[pl.ds(0, vmem.shape[0])]
    (pltpu.make_async_copy(vmem, rows, sem) if to_hbm
     else pltpu.make_async_copy(rows, vmem, sem)).wait()


def _expert_kernel(idx_prev, idx_cur, idx_next, gate_prev, gate_cur, wg_ref, wu_ref, wd_ref,
                   xn_hbm, yin_hbm, y_hbm, xbuf, xbf, pool, gcol, sem_x, sem_y, sem_s):
    del yin_hbm
    ybuf, sbuf, acc = pool.at[0], pool.at[1], pool.at[2]
    i = pl.program_id(0)
    f = pl.program_id(1)
    n_tiles = pl.num_programs(0)
    n_steps = pl.num_programs(1)
    tm = acc.shape[0]
    xr = xbuf.shape[1] // tm
    rows_per_step = tm * wd_ref.shape[0] // wd_ref.shape[1]
    slot = i % 2
    pslot = 1 - slot

    def gated_update(gate_ref):
        gcol[...] = jnp.broadcast_to(gate_ref[0], (LANES, tm)).T
        rb = 64
        def rows_pass(b, carry):
            rows = pl.ds(pl.multiple_of(b * rb, rb), rb)
            gate_col = gcol[rows, :]
            for k in range(acc.shape[1] // LANES):
                cols = slice(k * LANES, (k + 1) * LANES)
                sbuf[rows, cols] = ybuf[rows, cols] + acc[rows, cols] * gate_col
            return carry
        lax.fori_loop(0, tm // rb, rows_pass, 0)

    @pl.when((i == 0) & (f == 0))
    def _():
        sbuf[...] = jnp.zeros(sbuf.shape, sbuf.dtype)
        _start_rows_loop(xn_hbm, idx_cur, xbuf.at[0], tm, xr, sem_x.at[0], to_hbm=False)

    @pl.when(f == 0)
    def _():
        @pl.when(i > 0)
        def _():
            _wait_rows(y_hbm, ybuf, sem_y, to_hbm=False)
            _wait_rows(y_hbm, sbuf, sem_s, to_hbm=True)
            gated_update(gate_prev)

        _wait_rows(xn_hbm, xbuf.at[slot], sem_x.at[slot], to_hbm=False)
        for c, w in enumerate(_from_token_major(xbuf.at[slot], tm)):
            lo, hi = (pltpu.unpack_elementwise(w, index=k, packed_dtype=jnp.bfloat16,
                                               unpacked_dtype=jnp.float32) for k in (0, 1))
            xbf[:, 2 * c * LANES:(2 * c + 1) * LANES] = lo.astype(jnp.bfloat16)
            xbf[:, (2 * c + 1) * LANES:(2 * c + 2) * LANES] = hi.astype(jnp.bfloat16)

    r0 = f * rows_per_step
    copies = (
        functools.partial(_start_rows, xn_hbm, idx_next, xbuf.at[pslot], r0, rows_per_step, xr,
                          sem_x.at[pslot], to_hbm=False),
        functools.partial(_start_rows, y_hbm, idx_cur, ybuf, r0, rows_per_step, 1, sem_y,
                          to_hbm=False),
        functools.partial(_start_rows, y_hbm, idx_prev, sbuf, r0, rows_per_step, 1, sem_s,
                          to_hbm=True),
    )
    blocks = len(copies) + 1
    mb = tm // blocks
    wg, wu, wd = (w[...].astype(jnp.bfloat16) for w in (wg_ref, wu_ref, wd_ref))
    for j in range(blocks):
        rows = slice(j * mb, (j + 1) * mb)
        xb = xbf[rows, :]
        g = _dot(xb, wg)
        u = _dot(xb, wu)
        hcol = (g * (1.0 / (1.0 + jnp.exp(-g))) * u).astype(jnp.bfloat16)
        if j < len(copies):
            copies[j]()
        acc[rows, :] = _dot(hcol, wd) + jnp.where(f > 0, acc[rows, :], 0.0)

    @pl.when((i == n_tiles - 1) & (f == n_steps - 1))
    def _():
        _wait_rows(y_hbm, ybuf, sem_y, to_hbm=False)
        _wait_rows(y_hbm, sbuf, sem_s, to_hbm=True)
        _wait_rows(xn_hbm, xbuf.at[pslot], sem_x.at[pslot], to_hbm=False)
        gated_update(gate_cur)
        _start_rows_loop(y_hbm, idx_cur, sbuf, tm, 1, sem_s, to_hbm=True)
        _wait_rows(y_hbm, sbuf, sem_s, to_hbm=True)


def _experts(idx_tiles, gate_tiles, wg, wu, wd, xn_all, y_all):
    ne, d, dff = wg.shape
    tm, tf = EXPERT_TM, EXPERT_TF
    n_tiles = idx_tiles.shape[0]
    per_expert = n_tiles // ne
    n_steps = dff // tf
    xr = d // (2 * LANES)
    n_tok = xn_all.shape[0] // xr
    assert tm % (4 * n_steps) == 0 and d == dff and y_all.shape[0] >= n_tok + tm
    rows = jnp.arange(tm, dtype=jnp.int32)[None]
    idx_ext = jnp.concatenate([rows + n_tok, idx_tiles, rows], axis=0)
    gate_ext = jnp.concatenate([jnp.zeros((1, tm), jnp.float32), gate_tiles], axis=0)
    idx_ext = idx_ext.reshape(n_tiles + 2, 1, tm)
    gate_ext = gate_ext.reshape(n_tiles + 1, 1, tm)
    any_spec = pl.BlockSpec(memory_space=pl.ANY)
    idx_spec = lambda k: pl.BlockSpec((1, 1, tm), lambda i, f: (i + k, 0, 0), memory_space=pltpu.SMEM)
    gate_spec = lambda k: pl.BlockSpec((1, 1, tm), lambda i, f: (i + k, 0, 0))
    return pl.pallas_call(
        _expert_kernel,
        grid=(n_tiles, n_steps),
        in_specs=[
            idx_spec(0), idx_spec(1), idx_spec(2), gate_spec(0), gate_spec(1),
            pl.BlockSpec((None, d, tf), lambda i, f: (i // per_expert, 0, f)),
            pl.BlockSpec((None, d, tf), lambda i, f: (i // per_expert, 0, f)),
            pl.BlockSpec((None, tf, d), lambda i, f: (i // per_expert, f, 0)),
            any_spec, any_spec,
        ],
        out_specs=any_spec,
        out_shape=jax.ShapeDtypeStruct(y_all.shape, y_all.dtype),
        input_output_aliases={9: 0},
        scratch_shapes=[
            pltpu.VMEM((2, tm * xr, LANES), xn_all.dtype),
            pltpu.VMEM((tm, d), jnp.bfloat16),
            pltpu.VMEM((3, tm, d), jnp.float32),
            pltpu.VMEM((tm, LANES), jnp.float32),
            pltpu.SemaphoreType.DMA((2,)), pltpu.SemaphoreType.DMA, pltpu.SemaphoreType.DMA,
        ],
        compiler_params=pltpu.CompilerParams(
            dimension_semantics=("arbitrary", "arbitrary"), vmem_limit_bytes=VMEM_LIMIT),
        name="experts",
    )(idx_ext, idx_ext, idx_ext, gate_ext, gate_ext, wg, wu, wd, xn_all, y_all)


def _final_kernel(y_ref, g_ref, o_ref):
    o_ref[...] = _rms(y_ref[...], g_ref[...])


def _final_norm(y_all, g, row_offset, t):
    d = g.shape[1]
    tm = 512
    ob = row_offset // tm
    assert row_offset % tm == 0 and t % tm == 0
    return pl.pallas_call(
        _final_kernel,
        grid=(t // tm,),
        in_specs=[pl.BlockSpec((tm, d), lambda i: (i + ob, 0)),
                  pl.BlockSpec((1, d), lambda i: (0, 0))],
        out_specs=pl.BlockSpec((tm, d), lambda i: (i, 0)),
        out_shape=jax.ShapeDtypeStruct((t, d), jnp.float32),
        compiler_params=pltpu.CompilerParams(dimension_semantics=("arbitrary",)),
        name="final_norm",
    )(y_all, g)


def kernel(x_prompt, x_sample, norm1_g, w_in, v_norm_g, w_spatial, b_spatial, w_pool,
           b_pool, pool_scale, w_out, norm2_g, w_router, w_gate, w_up, w_down, final_norm_g):
    assert norm1_g.shape[0] == 1, "single-layer block"
    bf = jnp.bfloat16
    row = lambda a: a.reshape(1, -1)
    mixer_w = (row(norm1_g[0]), w_in[0].astype(bf), row(v_norm_g[0]), w_spatial[0].astype(bf),
               b_spatial[0].T, w_pool[0].astype(bf), b_pool[0], row(pool_scale[0]),
               w_out[0].astype(bf), row(norm2_g[0]), w_router[0].T.astype(bf))

    t_p = x_prompt.shape[0] * x_prompt.shape[1]
    t_s = x_sample.shape[0] * x_sample.shape[1]
    y_all, xn_all, aff = _mixer(x_prompt, x_sample, mixer_w, pad_rows=EXPERT_TM)
    idx_p, gate_p = _route(aff[:, :t_p], 0)
    idx_s, gate_s = _route(aff[:, t_p:], t_p)

    tiles = lambda a: jnp.concatenate([a[0], a[1]], axis=1).reshape(-1, EXPERT_TM)
    assert idx_p.shape[1] % EXPERT_TM == 0 and idx_s.shape[1] % EXPERT_TM == 0
    y_all = _experts(tiles((idx_p, idx_s)), tiles((gate_p, gate_s)),
                     w_gate[0], w_up[0], w_down[0], xn_all, y_all)

    fg = row(final_norm_g)
    return (_final_norm(y_all, fg, 0, t_p).reshape(x_prompt.shape),
            _final_norm(y_all, fg, t_p, t_s).reshape(x_sample.shape))
```

```python
import functools
import math

import jax
import jax.numpy as jnp
from jax import lax
from jax.experimental import pallas as pl
from jax.experimental.pallas import tpu as pltpu

EPS = 1e-6
CHUNK = 128
GMLP_HEADS = 4
POOL_WINDOWS = (2, 4, 8, 16)
HALO = 8
N_EXPERTS = 16
CAPACITY_FACTOR = 2

LANES = 128
BF16_ROWS = 16
MIXER_TM = 512
EXPERT_TM = 1024
EXPERT_TF = 256
ROUTE_W = 512
VMEM_LIMIT = 56 * 1024 * 1024


def _rms(x, g):
    return x * lax.rsqrt(jnp.mean(x * x, axis=-1, keepdims=True) + EPS) * g


def _gelu_tanh(x):
    c = math.sqrt(2.0 / math.pi)
    return 0.5 * x * (1.0 + jnp.tanh(c * (x + 0.044715 * (x * x * x))))


def _dot(a, b):
    return jnp.dot(a, b, preferred_element_type=jnp.float32)


def _to_token_major(ref, x):
    n, rows = x.shape[0], x.shape[1] // LANES
    for k in range(rows):
        ref[pl.ds(k, n, stride=rows), :] = x[:, k * LANES:(k + 1) * LANES]


def _from_token_major(ref, n):
    rows = ref.shape[0] // n
    for k in range(rows):
        yield ref[pl.ds(k, n, stride=rows), :]


def _mixer_kernel(xa_hbm, xb_hbm, g1_ref, win_ref, vg_ref, ws_ref, bs_ref, wp_ref, bp_ref,
                  ps_ref, wout_ref, g2_ref, wr_ref, x1_ref, xnorm_ref, aff_ref, mix_ref,
                  xbuf, sems, *, tiles_a, tiles_b, seq_a, seq_b):
    tm = x1_ref.shape[0]
    gw = vg_ref.shape[1]
    hd = gw // GMLP_HEADS
    pw = ps_ref.shape[1]
    gd = pw // len(POOL_WINDOWS)
    i = pl.program_id(0)
    in_a = i < tiles_a
    seq_len = jnp.where(in_a, seq_a, seq_b)
    tile = jnp.where(in_a, i, jnp.minimum(i - tiles_a, tiles_b - 1))
    s0 = lax.rem(tile * tm, seq_len)

    def tile_copies(hbm, t, slot):
        n = hbm.shape[0]
        starts = (t * tm, jnp.maximum(t * tm - HALO, 0), jnp.minimum((t + 1) * tm, n - HALO))
        sizes, offs = (tm, HALO, HALO), (0, tm, tm + HALO)
        return [pltpu.make_async_copy(hbm.at[pl.ds(pl.multiple_of(s, HALO), z)],
                                      xbuf.at[slot, pl.ds(o, z)], sems.at[slot, k])
                for k, (s, z, o) in enumerate(zip(starts, sizes, offs))]

    def start_fetch(j, slot):
        @pl.when(j < tiles_a)
        def _():
            for c in tile_copies(xa_hbm, j, slot):
                c.start()

        @pl.when(j >= tiles_a)
        def _():
            for c in tile_copies(xb_hbm, jnp.minimum(j - tiles_a, tiles_b - 1), slot):
                c.start()

    slot = i % 2

    @pl.when(i == 0)
    def _():
        start_fetch(i, slot)

    for c in tile_copies(xa_hbm, 0, slot):
        c.wait()

    @pl.when(i + 1 < pl.num_programs(0))
    def _():
        start_fetch(i + 1, 1 - slot)

    x = xbuf[slot, :tm]
    g1 = g1_ref[...]
    h_ext = _rms(xbuf[slot], g1).astype(jnp.bfloat16)

    z_p = _dot(h_ext, win_ref[:, 2 * gw:])
    z_uv = _gelu_tanh(_dot(h_ext[:tm], win_ref[:, :2 * gw]))

    u = z_uv[:, :gw]
    v = _rms(z_uv[:, gw:], vg_ref[...]).astype(jnp.bfloat16)
    bs = bs_ref[...]
    for c in range(tm // CHUNK):
        rows = slice(c * CHUNK, (c + 1) * CHUNK)
        for hh in range(GMLP_HEADS):
            cols = slice(hh * hd, (hh + 1) * hd)
            sv = _dot(ws_ref[hh], v[rows, cols]) + bs[:, hh:hh + 1]
            mix_ref[rows, cols] = (u[rows, cols] * sv).astype(jnp.bfloat16)

    p = z_p[:tm]
    n_ext = tm + 2 * HALO
    e_all = jnp.concatenate([jnp.where(s0 > 0, z_p[tm:tm + HALO], 0.0), p,
                             jnp.where(s0 + tm < seq_len, z_p[tm + HALO:], 0.0)], axis=0)
    shift_up = lambda a, k: pltpu.roll(a, n_ext - k, 0)
    rpos = lax.broadcasted_iota(jnp.int32, (tm, 1), 0) + s0
    for g, w in enumerate(POOL_WINDOWS):
        cols = slice(g * gd, (g + 1) * gd)
        run, k = e_all[:, cols], 1
        while k < w:
            run, k = run + shift_up(run, k), 2 * k
        win = (shift_up(run, HALO - w // 2) if w // 2 < HALO else run)[:tm]
        cnt = jnp.minimum(rpos + w // 2, seq_len) - jnp.maximum(rpos - w // 2, 0)
        d = win / cnt.astype(jnp.float32) - p[:, cols]
        yb = _dot(d.astype(jnp.bfloat16), wp_ref[g]) + bp_ref[g:g + 1, :]
        mix_ref[:, gw + g * gd:gw + (g + 1) * gd] = (yb * ps_ref[:, cols]).astype(jnp.bfloat16)

    x1 = x + _dot(mix_ref[...], wout_ref[...])
    x1_ref[...] = x1
    xn = _rms(x1, g2_ref[...])
    xb = xn.astype(jnp.bfloat16)
    words = [pltpu.pack_elementwise([xn[:, 2 * c * LANES:(2 * c + 1) * LANES],
                                     xn[:, (2 * c + 1) * LANES:(2 * c + 2) * LANES]],
                                    packed_dtype=jnp.bfloat16)
             for c in range(xn.shape[1] // (2 * LANES))]
    _to_token_major(xnorm_ref, jnp.concatenate(words, axis=1))
    logits = lax.dot_general(wr_ref[...], xb, (((1,), (1,)), ((), ())),
                             preferred_element_type=jnp.float32)
    e = jnp.exp(logits - jnp.max(logits, axis=0, keepdims=True))
    aff_ref[...] = e / jnp.sum(e, axis=0, keepdims=True)


def _mixer(xa, xb, weights, *, pad_rows):
    d = xa.shape[-1]
    tm = MIXER_TM
    seq_a, seq_b = xa.shape[1], xb.shape[1]
    ta, tb = xa.shape[0] * seq_a, xb.shape[0] * seq_b
    assert seq_a % tm == 0 and seq_b % tm == 0 and tm % CHUNK == 0 and pad_rows % tm == 0
    tiles_a, tiles_b = ta // tm, tb // tm
    n_tiles = tiles_a + tiles_b
    xr = d // (2 * LANES)
    const = lambda a: pl.BlockSpec(a.shape, lambda i: (0,) * a.ndim, pipeline_mode=pl.Buffered(1))
    any_spec = pl.BlockSpec(memory_space=pl.ANY)
    real = lambda i: jnp.minimum(i, n_tiles - 1)
    return pl.pallas_call(
        functools.partial(_mixer_kernel, tiles_a=tiles_a, tiles_b=tiles_b, seq_a=seq_a, seq_b=seq_b),
        grid=(n_tiles + pad_rows // tm,),
        in_specs=[any_spec, any_spec, *[const(w) for w in weights]],
        out_specs=[
            pl.BlockSpec((tm, d), lambda i: (i, 0)),
            pl.BlockSpec((tm * xr, LANES), lambda i: (real(i), 0)),
            pl.BlockSpec((N_EXPERTS, tm), lambda i: (0, real(i))),
        ],
        out_shape=[
            jax.ShapeDtypeStruct((ta + tb + pad_rows, d), jnp.float32),
            jax.ShapeDtypeStruct(((ta + tb) * xr, LANES), jnp.uint32),
            jax.ShapeDtypeStruct((N_EXPERTS, ta + tb), jnp.float32),
        ],
        scratch_shapes=[pltpu.VMEM((tm, weights[8].shape[0]), jnp.bfloat16),
                        pltpu.VMEM((2, tm + 2 * HALO, d), jnp.float32),
                        pltpu.SemaphoreType.DMA((2, 3))],
        compiler_params=pltpu.CompilerParams(
            dimension_semantics=("arbitrary",), vmem_limit_bytes=VMEM_LIMIT),
        name="mixer",
    )(xa.reshape(ta, d), xb.reshape(tb, d), *weights)


def _threshold_kernel(a_ref, thr_ref, *, cap):
    ne = a_ref.shape[0]

    def bisect(i, cur):
        cand = cur | jnp.left_shift(jnp.int32(1), 30 - i)
        cand_f = lax.bitcast_convert_type(cand, jnp.float32)
        n = jnp.sum(jnp.where(a_ref[...] >= cand_f, 1.0, 0.0), axis=1, keepdims=True)
        return jnp.where(n >= cap, cand, cur)
    thr = lax.fori_loop(0, 31, bisect, jnp.zeros((ne, 1), jnp.int32))
    thr_ref[...] = jnp.broadcast_to(lax.bitcast_convert_type(thr, jnp.float32), thr_ref.shape)


def _route_kernel(a_ref, thr_ref, idx_ref, gate_ref, *, cap, token_offset):
    a = a_ref[0]
    nb = a.shape[0]
    thr = thr_ref[0]

    def count(mask):
        s = jnp.sum(jnp.where(mask, 1.0, 0.0), axis=0, keepdims=True)
        return jnp.sum(s, axis=1, keepdims=True)

    jj = lax.broadcasted_iota(jnp.int32, (LANES, LANES), 0)
    kk = lax.broadcasted_iota(jnp.int32, (LANES, LANES), 1)
    upper = jnp.where(jj <= kk, 1.0, 0.0).astype(jnp.bfloat16)
    bi = lax.broadcasted_iota(jnp.int32, (nb, nb), 0)
    bk = lax.broadcasted_iota(jnp.int32, (nb, nb), 1)
    lower = jnp.where(bk <= bi, 1.0, 0.0).astype(jnp.bfloat16)

    def ranks(mask):
        m = jnp.where(mask, 1.0, 0.0).astype(jnp.bfloat16)
        local = _dot(m, upper)
        p_in = jnp.sum(_dot(lower, m), axis=1, keepdims=True)
        p_ex = p_in - local[:, LANES - 1:LANES]
        return local, p_in, p_ex

    gt = a > thr
    eq = a == thr
    need = cap - count(gt)
    eq_local, _, eq_pex = ranks(eq)
    sel = gt | (eq & (eq_pex + eq_local <= need))
    local, p_in, p_ex = ranks(sel)

    a_hi = a.astype(jnp.bfloat16)
    r1 = a - a_hi.astype(jnp.float32)
    a_mid = r1.astype(jnp.bfloat16)
    a_lo = (r1 - a_mid.astype(jnp.float32)).astype(jnp.bfloat16)
    local_t = local.T.astype(jnp.bfloat16)
    hi_t = a_hi.astype(jnp.float32).T.astype(jnp.bfloat16)
    mid_t = a_mid.astype(jnp.float32).T.astype(jnp.bfloat16)
    lo_t = a_lo.astype(jnp.float32).T.astype(jnp.bfloat16)
    jcol = lax.broadcasted_iota(jnp.int32, (LANES, 1), 0).astype(jnp.float32)

    w = idx_ref.shape[2]
    def resolve(ci, carry):
        c = (lax.broadcasted_iota(jnp.int32, (1, w), 1) + ci * w).astype(jnp.float32)
        in_blk = (p_ex <= c) & (c < p_in)
        onehot = jnp.where(in_blk, 1.0, 0.0).astype(jnp.bfloat16)
        blk = jnp.sum(jnp.where(p_in <= c, 1.0, 0.0), axis=0, keepdims=True)
        lc = c - jnp.sum(jnp.where(in_blk, p_ex, 0.0), axis=0, keepdims=True)
        g_local = _dot(local_t, onehot)
        jstar = jnp.sum(jnp.where(g_local <= lc, 1.0, 0.0), axis=0, keepdims=True)
        idx_ref[0, pl.ds(ci, 1), :] = (blk * LANES + jstar).astype(jnp.int32) + token_offset
        g_aff = (_dot(hi_t, onehot) + _dot(mid_t, onehot)) + _dot(lo_t, onehot)
        gate_ref[0, pl.ds(ci, 1), :] = jnp.sum(
            jnp.where(jcol == jstar, g_aff, 0.0), axis=0, keepdims=True)
        return carry
    lax.fori_loop(0, cap // w, resolve, 0)


def _route(aff_t, token_offset):
    ne, t = aff_t.shape
    cap = CAPACITY_FACTOR * t // ne
    nb = t // LANES
    w = ROUTE_W
    assert t % LANES == 0 and nb % 8 == 0 and cap % w == 0
    thr = pl.pallas_call(
        functools.partial(_threshold_kernel, cap=cap),
        out_shape=jax.ShapeDtypeStruct((ne, LANES), jnp.float32),
        name="threshold",
    )(aff_t)
    idx, gate = pl.pallas_call(
        functools.partial(_route_kernel, cap=cap, token_offset=token_offset),
        grid=(ne,),
        in_specs=[pl.BlockSpec((1, nb, LANES), lambda e: (e, 0, 0)),
                  pl.BlockSpec((1, 1, LANES), lambda e: (e, 0, 0))],
        out_specs=[pl.BlockSpec((1, cap // w, w), lambda e: (e, 0, 0)),
                   pl.BlockSpec((1, cap // w, w), lambda e: (e, 0, 0))],
        out_shape=[jax.ShapeDtypeStruct((ne, cap // w, w), jnp.int32),
                   jax.ShapeDtypeStruct((ne, cap // w, w), jnp.float32)],
        compiler_params=pltpu.CompilerParams(dimension_semantics=("arbitrary",)),
        name="route",
    )(aff_t.reshape(ne, nb, LANES), thr.reshape(ne, 1, LANES))
    return idx.reshape(ne, cap), gate.reshape(ne, cap)


def _token_copy(hbm, vmem, tok, slot, rows, sem, to_hbm):
    src, dst = hbm.at[pl.ds(tok * rows, rows)], vmem.at[pl.ds(slot * rows, rows)]
    return pltpu.make_async_copy(dst, src, sem) if to_hbm else pltpu.make_async_copy(src, dst, sem)


DMA_QUEUES = 2


def _start_rows(hbm, idx_ref, vmem, r0, n, rows, sem, *, to_hbm):
    for r in range(n):
        _token_copy(hbm, vmem, idx_ref[0, 0, r0 + r], r0 + r, rows, sem, to_hbm).start(
            priority=r % DMA_QUEUES)


def _start_rows_loop(hbm, idx_ref, vmem, n, rows, sem, *, to_hbm):
    def issue(r, c):
        _token_copy(hbm, vmem, idx_ref[0, 0, r], r, rows, sem, to_hbm).start()
        return c
    lax.fori_loop(0, n, issue, 0, unroll=8)


def _wait_rows(hbm, vmem, sem, *, to_hbm):
    rows = hbm.at[pl.ds(0, vmem.shape[0])]
    (pltpu.make_async_copy(vmem, rows, sem) if to_hbm
     else pltpu.make_async_copy(rows, vmem, sem)).wait()


def _expert_kernel(idx_prev, idx_cur, idx_next, gate_prev, gate_cur, wg_ref, wu_ref, wd_ref,
                   xn_hbm, yin_hbm, y_hbm, xbuf, xbf, pool, gcol, sem_x, sem_y, sem_s):
    del yin_hbm
    ybuf, sbuf, acc = pool.at[0], pool.at[1], pool.at[2]
    i = pl.program_id(0)
    f = pl.program_id(1)
    n_tiles = pl.num_programs(0)
    n_steps = pl.num_programs(1)
    tm = acc.shape[0]
    xr = xbuf.shape[1] // tm
    rows_per_step = tm * wd_ref.shape[0] // wd_ref.shape[1]
    slot = i % 2
    pslot = 1 - slot

    def gated_update(gate_ref):
        gcol[...] = jnp.broadcast_to(gate_ref[0], (LANES, tm)).T
        rb = 64
        def rows_pass(b, carry):
            rows = pl.ds(pl.multiple_of(b * rb, rb), rb)
            gate_col = gcol[rows, :]
            for k in range(acc.shape[1] // LANES):
                cols = slice(k * LANES, (k + 1) * LANES)
                sbuf[rows, cols] = ybuf[rows, cols] + acc[rows, cols] * gate_col
            return carry
        lax.fori_loop(0, tm // rb, rows_pass, 0)

    @pl.when((i == 0) & (f == 0))
    def _():
        sbuf[...] = jnp.zeros(sbuf.shape, sbuf.dtype)
        _start_rows_loop(xn_hbm, idx_cur, xbuf.at[0], tm, xr, sem_x.at[0], to_hbm=False)

    @pl.when(f == 0)
    def _():
        @pl.when(i > 0)
        def _():
            _wait_rows(y_hbm, ybuf, sem_y, to_hbm=False)
            _wait_rows(y_hbm, sbuf, sem_s, to_hbm=True)
            gated_update(gate_prev)

        _wait_rows(xn_hbm, xbuf.at[slot], sem_x.at[slot], to_hbm=False)
        for c, w in enumerate(_from_token_major(xbuf.at[slot], tm)):
            lo, hi = (pltpu.unpack_elementwise(w, index=k, packed_dtype=jnp.bfloat16,
                                               unpacked_dtype=jnp.float32) for k in (0, 1))
            xbf[:, 2 * c * LANES:(2 * c + 1) * LANES] = lo.astype(jnp.bfloat16)
            xbf[:, (2 * c + 1) * LANES:(2 * c + 2) * LANES] = hi.astype(jnp.bfloat16)

    r0 = f * rows_per_step
    copies = (
        functools.partial(_start_rows, xn_hbm, idx_next, xbuf.at[pslot], r0, rows_per_step, xr,
                          sem_x.at[pslot], to_hbm=False),
        functools.partial(_start_rows, y_hbm, idx_cur, ybuf, r0, rows_per_step, 1, sem_y,
                          to_hbm=False),
        functools.partial(_start_rows, y_hbm, idx_prev, sbuf, r0, rows_per_step, 1, sem_s,
                          to_hbm=True),
    )
    blocks = len(copies) + 1
    mb = tm // blocks
    wg, wu, wd = (w[...].astype(jnp.bfloat16) for w in (wg_ref, wu_ref, wd_ref))
    for j in range(blocks):
        rows = slice(j * mb, (j + 1) * mb)
        xb = xbf[rows, :]
        g = _dot(xb, wg)
        u = _dot(xb, wu)
        hcol = (g * (1.0 / (1.0 + jnp.exp(-g))) * u).astype(jnp.bfloat16)
        if j < len(copies):
            copies[j]()
        acc[rows, :] = _dot(hcol, wd) + jnp.where(f > 0, acc[rows, :], 0.0)

    @pl.when((i == n_tiles - 1) & (f == n_steps - 1))
    def _():
        _wait_rows(y_hbm, ybuf, sem_y, to_hbm=False)
        _wait_rows(y_hbm, sbuf, sem_s, to_hbm=True)
        _wait_rows(xn_hbm, xbuf.at[pslot], sem_x.at[pslot], to_hbm=False)
        gated_update(gate_cur)
        _start_rows_loop(y_hbm, idx_cur, sbuf, tm, 1, sem_s, to_hbm=True)
        _wait_rows(y_hbm, sbuf, sem_s, to_hbm=True)


def _experts(idx_tiles, gate_tiles, wg, wu, wd, xn_all, y_all):
    ne, d, dff = wg.shape
    tm, tf = EXPERT_TM, EXPERT_TF
    n_tiles = idx_tiles.shape[0]
    per_expert = n_tiles // ne
    n_steps = dff // tf
    xr = d // (2 * LANES)
    n_tok = xn_all.shape[0] // xr
    assert tm % (4 * n_steps) == 0 and d == dff and y_all.shape[0] >= n_tok + tm
    rows = jnp.arange(tm, dtype=jnp.int32)[None]
    idx_ext = jnp.concatenate([rows + n_tok, idx_tiles, rows], axis=0)
    gate_ext = jnp.concatenate([jnp.zeros((1, tm), jnp.float32), gate_tiles], axis=0)
    idx_ext = idx_ext.reshape(n_tiles + 2, 1, tm)
    gate_ext = gate_ext.reshape(n_tiles + 1, 1, tm)
    any_spec = pl.BlockSpec(memory_space=pl.ANY)
    idx_spec = lambda k: pl.BlockSpec((1, 1, tm), lambda i, f: (i + k, 0, 0), memory_space=pltpu.SMEM)
    gate_spec = lambda k: pl.BlockSpec((1, 1, tm), lambda i, f: (i + k, 0, 0))
    return pl.pallas_call(
        _expert_kernel,
        grid=(n_tiles, n_steps),
        in_specs=[
            idx_spec(0), idx_spec(1), idx_spec(2), gate_spec(0), gate_spec(1),
            pl.BlockSpec((None, d, tf), lambda i, f: (i // per_expert, 0, f)),
            pl.BlockSpec((None, d, tf), lambda i, f: (i // per_expert, 0, f)),
            pl.BlockSpec((None, tf, d), lambda i, f: (i // per_expert, f, 0)),
            any_spec, any_spec,
        ],
        out_specs=any_spec,
        out_shape=jax.ShapeDtypeStruct(y_all.shape, y_all.dtype),
        input_output_aliases={9: 0},
        scratch_shapes=[
            pltpu.VMEM((2, tm * xr, LANES), xn_all.dtype),
            pltpu.VMEM((tm, d), jnp.bfloat16),
            pltpu.VMEM((3, tm, d), jnp.float32),
            pltpu.VMEM((tm, LANES), jnp.float32),
            pltpu.SemaphoreType.DMA((2,)), pltpu.SemaphoreType.DMA, pltpu.SemaphoreType.DMA,
        ],
        compiler_params=pltpu.CompilerParams(
            dimension_semantics=("arbitrary", "arbitrary"), vmem_limit_bytes=VMEM_LIMIT),
        name="experts",
    )(idx_ext, idx_ext, idx_ext, gate_ext, gate_ext, wg, wu, wd, xn_all, y_all)


def _final_kernel(y_ref, g_ref, o_ref):
    o_ref[...] = _rms(y_ref[...], g_ref[...])


def _final_norm(y_all, g, row_offset, t):
    d = g.shape[1]
    tm = 512
    ob = row_offset // tm
    assert row_offset % tm == 0 and t % tm == 0
    return pl.pallas_call(
        _final_kernel,
        grid=(t // tm,),
        in_specs=[pl.BlockSpec((tm, d), lambda i: (i + ob, 0)),
                  pl.BlockSpec((1, d), lambda i: (0, 0))],
        out_specs=pl.BlockSpec((tm, d), lambda i: (i, 0)),
        out_shape=jax.ShapeDtypeStruct((t, d), jnp.float32),
        compiler_params=pltpu.CompilerParams(dimension_semantics=("arbitrary",)),
        name="final_norm",
    )(y_all, g)


def kernel(x_prompt, x_sample, norm1_g, w_in, v_norm_g, w_spatial, b_spatial, w_pool,
           b_pool, pool_scale, w_out, norm2_g, w_router, w_gate, w_up, w_down, final_norm_g):
    assert norm1_g.shape[0] == 1, "single-layer block"
    bf = jnp.bfloat16
    row = lambda a: a.reshape(1, -1)
    mixer_w = (row(norm1_g[0]), w_in[0].astype(bf), row(v_norm_g[0]), w_spatial[0].astype(bf),
               b_spatial[0].T, w_pool[0].astype(bf), b_pool[0], row(pool_scale[0]),
               w_out[0].astype(bf), row(norm2_g[0]), w_router[0].T.astype(bf))

    t_p = x_prompt.shape[0] * x_prompt.shape[1]
    t_s = x_sample.shape[0] * x_sample.shape[1]
    y_all, xn_all, aff = _mixer(x_prompt, x_sample, mixer_w, pad_rows=EXPERT_TM)
    idx_p, gate_p = _route(aff[:, :t_p], 0)
    idx_s, gate_s = _route(aff[:, t_p:], t_p)

    tiles = lambda a: jnp.concatenate([a[0], a[1]], axis=1).reshape(-1, EXPERT_TM)
    assert idx_p.shape[1] % EXPERT_TM == 0 and idx_s.shape[1] % EXPERT_TM == 0
    y_all = _experts(tiles((idx_p, idx_s)), tiles((gate_p, gate_s)),
                     w_gate[0], w_up[0], w_down[0], xn_all, y_all)

    fg = row(final_norm_g)
    return (_final_norm(y_all, fg, 0, t_p).reshape(x_prompt.shape),
            _final_norm(y_all, fg, t_p, t_s).reshape(x_sample.shape))
```

```python
import functools
import math

import jax
import jax.numpy as jnp
from jax import lax
from jax.experimental import pallas as pl
from jax.experimental.pallas import tpu as pltpu

EPS = 1e-6
CHUNK = 128
GMLP_HEADS = 4
POOL_WINDOWS = (2, 4, 8, 16)
HALO = 8
N_EXPERTS = 16
CAPACITY_FACTOR = 2

LANES = 128
BF16_ROWS = 16
MIXER_TM = 512
EXPERT_TM = 1024
EXPERT_TF = 256
ROUTE_W = 1024
VMEM_LIMIT = 56 * 1024 * 1024


def _rms(x, g):
    return x * lax.rsqrt(jnp.mean(x * x, axis=-1, keepdims=True) + EPS) * g


def _gelu_tanh(x):
    c = math.sqrt(2.0 / math.pi)
    return 0.5 * x * (1.0 + jnp.tanh(c * (x + 0.044715 * (x * x * x))))


def _dot(a, b):
    return jnp.dot(a, b, preferred_element_type=jnp.float32)


def _to_token_major(ref, x):
    n, rows = x.shape[0], x.shape[1] // LANES
    for k in range(rows):
        ref[pl.ds(k, n, stride=rows), :] = x[:, k * LANES:(k + 1) * LANES]


def _from_token_major(ref, n):
    rows = ref.shape[0] // n
    for k in range(rows):
        yield ref[pl.ds(k, n, stride=rows), :]


def _mixer_kernel(xa_hbm, xb_hbm, g1_ref, win_ref, vg_ref, ws_ref, bs_ref, wp_ref, bp_ref,
                  ps_ref, wout_ref, g2_ref, wr_ref, x1_ref, xnorm_ref, aff_ref, mix_ref,
                  xbuf, sems, *, tiles_a, tiles_b, seq_a, seq_b):
    tm = x1_ref.shape[0]
    gw = vg_ref.shape[1]
    hd = gw // GMLP_HEADS
    pw = ps_ref.shape[1]
    gd = pw // len(POOL_WINDOWS)
    i = pl.program_id(0)
    in_a = i < tiles_a
    seq_len = jnp.where(in_a, seq_a, seq_b)
    tile = jnp.where(in_a, i, jnp.minimum(i - tiles_a, tiles_b - 1))
    s0 = lax.rem(tile * tm, seq_len)

    def tile_copies(hbm, t, slot):
        n = hbm.shape[0]
        starts = (t * tm, jnp.maximum(t * tm - HALO, 0), jnp.minimum((t + 1) * tm, n - HALO))
        sizes, offs = (tm, HALO, HALO), (0, tm, tm + HALO)
        return [pltpu.make_async_copy(hbm.at[pl.ds(pl.multiple_of(s, HALO), z)],
                                      xbuf.at[slot, pl.ds(o, z)], sems.at[slot, k])
                for k, (s, z, o) in enumerate(zip(starts, sizes, offs))]

    def start_fetch(j, slot):
        @pl.when(j < tiles_a)
        def _():
            for c in tile_copies(xa_hbm, j, slot):
                c.start()

        @pl.when(j >= tiles_a)
        def _():
            for c in tile_copies(xb_hbm, jnp.minimum(j - tiles_a, tiles_b - 1), slot):
                c.start()

    slot = i % 2

    @pl.when(i == 0)
    def _():
        start_fetch(i, slot)

    for c in tile_copies(xa_hbm, 0, slot):
        c.wait()

    @pl.when(i + 1 < pl.num_programs(0))
    def _():
        start_fetch(i + 1, 1 - slot)

    x = xbuf[slot, :tm]
    g1 = g1_ref[...]
    h_ext = _rms(xbuf[slot], g1).astype(jnp.bfloat16)

    z_p = _dot(h_ext, win_ref[:, 2 * gw:])
    z_uv = _gelu_tanh(_dot(h_ext[:tm], win_ref[:, :2 * gw]))

    u = z_uv[:, :gw]
    v = _rms(z_uv[:, gw:], vg_ref[...]).astype(jnp.bfloat16)
    bs = bs_ref[...]
    for c in range(tm // CHUNK):
        rows = slice(c * CHUNK, (c + 1) * CHUNK)
        for hh in range(GMLP_HEADS):
            cols = slice(hh * hd, (hh + 1) * hd)
            sv = _dot(ws_ref[hh], v[rows, cols]) + bs[:, hh:hh + 1]
            mix_ref[rows, cols] = (u[rows, cols] * sv).astype(jnp.bfloat16)

    p = z_p[:tm]
    n_ext = tm + 2 * HALO
    e_all = jnp.concatenate([jnp.where(s0 > 0, z_p[tm:tm + HALO], 0.0), p,
                             jnp.where(s0 + tm < seq_len, z_p[tm + HALO:], 0.0)], axis=0)
    shift_up = lambda a, k: pltpu.roll(a, n_ext - k, 0)
    rpos = lax.broadcasted_iota(jnp.int32, (tm, 1), 0) + s0
    for g, w in enumerate(POOL_WINDOWS):
        cols = slice(g * gd, (g + 1) * gd)
        run, k = e_all[:, cols], 1
        while k < w:
            run, k = run + shift_up(run, k), 2 * k
        win = (shift_up(run, HALO - w // 2) if w // 2 < HALO else run)[:tm]
        cnt = jnp.minimum(rpos + w // 2, seq_len) - jnp.maximum(rpos - w // 2, 0)
        d = win / cnt.astype(jnp.float32) - p[:, cols]
        yb = _dot(d.astype(jnp.bfloat16), wp_ref[g]) + bp_ref[g:g + 1, :]
        mix_ref[:, gw + g * gd:gw + (g + 1) * gd] = (yb * ps_ref[:, cols]).astype(jnp.bfloat16)

    x1 = x + _dot(mix_ref[...], wout_ref[...])
    x1_ref[...] = x1
    xn = _rms(x1, g2_ref[...])
    xb = xn.astype(jnp.bfloat16)
    words = [pltpu.pack_elementwise([xn[:, 2 * c * LANES:(2 * c + 1) * LANES],
                                     xn[:, (2 * c + 1) * LANES:(2 * c + 2) * LANES]],
                                    packed_dtype=jnp.bfloat16)
             for c in range(xn.shape[1] // (2 * LANES))]
    _to_token_major(xnorm_ref, jnp.concatenate(words, axis=1))
    logits = lax.dot_general(wr_ref[...], xb, (((1,), (1,)), ((), ())),
                             preferred_element_type=jnp.float32)
    e = jnp.exp(logits - jnp.max(logits, axis=0, keepdims=True))
    aff_ref[...] = e / jnp.sum(e, axis=0, keepdims=True)


def _mixer(xa, xb, weights, *, pad_rows):
    d = xa.shape[-1]
    tm = MIXER_TM
    seq_a, seq_b = xa.shape[1], xb.shape[1]
    ta, tb = xa.shape[0] * seq_a, xb.shape[0] * seq_b
    assert seq_a % tm == 0 and seq_b % tm == 0 and tm % CHUNK == 0 and pad_rows % tm == 0
    tiles_a, tiles_b = ta // tm, tb // tm
    n_tiles = tiles_a + tiles_b
    xr = d // (2 * LANES)
    const = lambda a: pl.BlockSpec(a.shape, lambda i: (0,) * a.ndim, pipeline_mode=pl.Buffered(1))
    any_spec = pl.BlockSpec(memory_space=pl.ANY)
    real = lambda i: jnp.minimum(i, n_tiles - 1)
    return pl.pallas_call(
        functools.partial(_mixer_kernel, tiles_a=tiles_a, tiles_b=tiles_b, seq_a=seq_a, seq_b=seq_b),
        grid=(n_tiles + pad_rows // tm,),
        in_specs=[any_spec, any_spec, *[const(w) for w in weights]],
        out_specs=[
            pl.BlockSpec((tm, d), lambda i: (i, 0)),
            pl.BlockSpec((tm * xr, LANES), lambda i: (real(i), 0)),
            pl.BlockSpec((N_EXPERTS, tm), lambda i: (0, real(i))),
        ],
        out_shape=[
            jax.ShapeDtypeStruct((ta + tb + pad_rows, d), jnp.float32),
            jax.ShapeDtypeStruct(((ta + tb) * xr, LANES), jnp.uint32),
            jax.ShapeDtypeStruct((N_EXPERTS, ta + tb), jnp.float32),
        ],
        scratch_shapes=[pltpu.VMEM((tm, weights[8].shape[0]), jnp.bfloat16),
                        pltpu.VMEM((2, tm + 2 * HALO, d), jnp.float32),
                        pltpu.SemaphoreType.DMA((2, 3))],
        compiler_params=pltpu.CompilerParams(
            dimension_semantics=("arbitrary",), vmem_limit_bytes=VMEM_LIMIT),
        name="mixer",
    )(xa.reshape(ta, d), xb.reshape(tb, d), *weights)


def _threshold_kernel(a_ref, thr_ref, *, cap):
    ne = a_ref.shape[0]

    def bisect(i, cur):
        cand = cur | jnp.left_shift(jnp.int32(1), 30 - i)
        cand_f = lax.bitcast_convert_type(cand, jnp.float32)
        n = jnp.sum(jnp.where(a_ref[...] >= cand_f, 1.0, 0.0), axis=1, keepdims=True)
        return jnp.where(n >= cap, cand, cur)
    thr = lax.fori_loop(0, 31, bisect, jnp.zeros((ne, 1), jnp.int32))
    thr_ref[...] = jnp.broadcast_to(lax.bitcast_convert_type(thr, jnp.float32), thr_ref.shape)


def _route_kernel(a_ref, thr_ref, idx_ref, gate_ref, *, cap, token_offset):
    a = a_ref[0]
    nb = a.shape[0]
    thr = thr_ref[0]

    def count(mask):
        s = jnp.sum(jnp.where(mask, 1.0, 0.0), axis=0, keepdims=True)
        return jnp.sum(s, axis=1, keepdims=True)

    jj = lax.broadcasted_iota(jnp.int32, (LANES, LANES), 0)
    kk = lax.broadcasted_iota(jnp.int32, (LANES, LANES), 1)
    upper = jnp.where(jj <= kk, 1.0, 0.0).astype(jnp.bfloat16)
    bi = lax.broadcasted_iota(jnp.int32, (nb, nb), 0)
    bk = lax.broadcasted_iota(jnp.int32, (nb, nb), 1)
    lower = jnp.where(bk <= bi, 1.0, 0.0).astype(jnp.bfloat16)

    def ranks(mask):
        m = jnp.where(mask, 1.0, 0.0).astype(jnp.bfloat16)
        local = _dot(m, upper)
        p_in = jnp.sum(_dot(lower, m), axis=1, keepdims=True)
        p_ex = p_in - local[:, LANES - 1:LANES]
        return local, p_in, p_ex

    gt = a > thr
    eq = a == thr
    need = cap - count(gt)
    eq_local, _, eq_pex = ranks(eq)
    sel = gt | (eq & (eq_pex + eq_local <= need))
    local, p_in, p_ex = ranks(sel)

    a_hi = a.astype(jnp.bfloat16)
    r1 = a - a_hi.astype(jnp.float32)
    a_mid = r1.astype(jnp.bfloat16)
    a_lo = (r1 - a_mid.astype(jnp.float32)).astype(jnp.bfloat16)
    local_t = local.T.astype(jnp.bfloat16)
    hi_t = a_hi.astype(jnp.float32).T.astype(jnp.bfloat16)
    mid_t = a_mid.astype(jnp.float32).T.astype(jnp.bfloat16)
    lo_t = a_lo.astype(jnp.float32).T.astype(jnp.bfloat16)
    jcol = lax.broadcasted_iota(jnp.int32, (LANES, 1), 0).astype(jnp.float32)

    w = idx_ref.shape[2]
    def resolve(ci, carry):
        c = (lax.broadcasted_iota(jnp.int32, (1, w), 1) + ci * w).astype(jnp.float32)
        in_blk = (p_ex <= c) & (c < p_in)
        onehot = jnp.where(in_blk, 1.0, 0.0).astype(jnp.bfloat16)
        blk = jnp.sum(jnp.where(p_in <= c, 1.0, 0.0), axis=0, keepdims=True)
        lc = c - jnp.sum(jnp.where(in_blk, p_ex, 0.0), axis=0, keepdims=True)
        g_local = _dot(local_t, onehot)
        jstar = jnp.sum(jnp.where(g_local <= lc, 1.0, 0.0), axis=0, keepdims=True)
        idx_ref[0, pl.ds(ci, 1), :] = (blk * LANES + jstar).astype(jnp.int32) + token_offset
        g_aff = (_dot(hi_t, onehot) + _dot(mid_t, onehot)) + _dot(lo_t, onehot)
        gate_ref[0, pl.ds(ci, 1), :] = jnp.sum(
            jnp.where(jcol == jstar, g_aff, 0.0), axis=0, keepdims=True)
        return carry
    lax.fori_loop(0, cap // w, resolve, 0)


def _route(aff_t, token_offset):
    ne, t = aff_t.shape
    cap = CAPACITY_FACTOR * t // ne
    nb = t // LANES
    w = ROUTE_W
    assert t % LANES == 0 and nb % 8 == 0 and cap % w == 0
    thr = pl.pallas_call(
        functools.partial(_threshold_kernel, cap=cap),
        out_shape=jax.ShapeDtypeStruct((ne, LANES), jnp.float32),
        name="threshold",
    )(aff_t)
    idx, gate = pl.pallas_call(
        functools.partial(_route_kernel, cap=cap, token_offset=token_offset),
        grid=(ne,),
        in_specs=[pl.BlockSpec((1, nb, LANES), lambda e: (e, 0, 0)),
                  pl.BlockSpec((1, 1, LANES), lambda e: (e, 0, 0))],
        out_specs=[pl.BlockSpec((1, cap // w, w), lambda e: (e, 0, 0)),
                   pl.BlockSpec((1, cap // w, w), lambda e: (e, 0, 0))],
        out_shape=[jax.ShapeDtypeStruct((ne, cap // w, w), jnp.int32),
                   jax.ShapeDtypeStruct((ne, cap // w, w), jnp.float32)],
        compiler_params=pltpu.CompilerParams(dimension_semantics=("arbitrary",)),
        name="route",
    )(aff_t.reshape(ne, nb, LANES), thr.reshape(ne, 1, LANES))
    return idx.reshape(ne, cap), gate.reshape(ne, cap)


def _token_copy(hbm, vmem, tok, slot, rows, sem, to_hbm):
    src, dst = hbm.at[pl.ds(tok * rows, rows)], vmem.at[pl.ds(slot * rows, rows)]
    return pltpu.make_async_copy(dst, src, sem) if to_hbm else pltpu.make_async_copy(src, dst, sem)


DMA_QUEUES = 2


def _start_rows(hbm, idx_ref, vmem, r0, n, rows, sem, *, to_hbm):
    for r in range(n):
        _token_copy(hbm, vmem, idx_ref[0, 0, r0 + r], r0 + r, rows, sem, to_hbm).start(
            priority=r % DMA_QUEUES)


def _start_rows_loop(hbm, idx_ref, vmem, n, rows, sem, *, to_hbm):
    def issue(r, c):
        _token_copy(hbm, vmem, idx_ref[0, 0, r], r, rows, sem, to_hbm).start()
        return c
    lax.fori_loop(0, n, issue, 0, unroll=8)


def _wait_rows(hbm, vmem, sem, *, to_hbm):
    rows = hbm.at[pl.ds(0, vmem.shape[0])]
    (pltpu.make_async_copy(vmem, rows, sem) if to_hbm
     else pltpu.make_async_copy(rows, vmem, sem)).wait()


def _expert_kernel(idx_prev, idx_cur, idx_next, gate_prev, gate_cur, wg_ref, wu_ref, wd_ref,
                   xn_hbm, yin_hbm, y_hbm, xbuf, xbf, pool, gcol, sem_x, sem_y, sem_s):
    del yin_hbm
    ybuf, sbuf, acc = pool.at[0], pool.at[1], pool.at[2]
    i = pl.program_id(0)
    f = pl.program_id(1)
    n_tiles = pl.num_programs(0)
    n_steps = pl.num_programs(1)
    tm = acc.shape[0]
    xr = xbuf.shape[1] // tm
    rows_per_step = tm * wd_ref.shape[0] // wd_ref.shape[1]
    slot = i % 2
    pslot = 1 - slot

    def gated_update(gate_ref):
        gcol[...] = jnp.broadcast_to(gate_ref[0], (LANES, tm)).T
        rb = 64
        def rows_pass(b, carry):
            rows = pl.ds(pl.multiple_of(b * rb, rb), rb)
            gate_col = gcol[rows, :]
            for k in range(acc.shape[1] // LANES):
                cols = slice(k * LANES, (k + 1) * LANES)
                sbuf[rows, cols] = ybuf[rows, cols] + acc[rows, cols] * gate_col
            return carry
        lax.fori_loop(0, tm // rb, rows_pass, 0)

    @pl.when((i == 0) & (f == 0))
    def _():
        sbuf[...] = jnp.zeros(sbuf.shape, sbuf.dtype)
        _start_rows_loop(xn_hbm, idx_cur, xbuf.at[0], tm, xr, sem_x.at[0], to_hbm=False)

    @pl.when(f == 0)
    def _():
        @pl.when(i > 0)
        def _():
            _wait_rows(y_hbm, ybuf, sem_y, to_hbm=False)
            _wait_rows(y_hbm, sbuf, sem_s, to_hbm=True)
            gated_update(gate_prev)

        _wait_rows(xn_hbm, xbuf.at[slot], sem_x.at[slot], to_hbm=False)
        for c, w in enumerate(_from_token_major(xbuf.at[slot], tm)):
            lo, hi = (pltpu.unpack_elementwise(w, index=k, packed_dtype=jnp.bfloat16,
                                               unpacked_dtype=jnp.float32) for k in (0, 1))
            xbf[:, 2 * c * LANES:(2 * c + 1) * LANES] = lo.astype(jnp.bfloat16)
            xbf[:, (2 * c + 1) * LANES:(2 * c + 2) * LANES] = hi.astype(jnp.bfloat16)

    r0 = f * rows_per_step
    copies = (
        functools.partial(_start_rows, xn_hbm, idx_next, xbuf.at[pslot], r0, rows_per_step, xr,
                          sem_x.at[pslot], to_hbm=False),
        functools.partial(_start_rows, y_hbm, idx_cur, ybuf, r0, rows_per_step, 1, sem_y,
                          to_hbm=False),
        functools.partial(_start_rows, y_hbm, idx_prev, sbuf, r0, rows_per_step, 1, sem_s,
                          to_hbm=True),
    )
    blocks = len(copies) + 1
    mb = tm // blocks
    wg, wu, wd = (w[...].astype(jnp.bfloat16) for w in (wg_ref, wu_ref, wd_ref))
    for j in range(blocks):
        rows = slice(j * mb, (j + 1) * mb)
        xb = xbf[rows, :]
        g = _dot(xb, wg)
        u = _dot(xb, wu)
        hcol = (g * (1.0 / (1.0 + jnp.exp(-g))) * u).astype(jnp.bfloat16)
        if j < len(copies):
            copies[j]()
        acc[rows, :] = _dot(hcol, wd) + jnp.where(f > 0, acc[rows, :], 0.0)

    @pl.when((i == n_tiles - 1) & (f == n_steps - 1))
    def _():
        _wait_rows(y_hbm, ybuf, sem_y, to_hbm=False)
        _wait_rows(y_hbm, sbuf, sem_s, to_hbm=True)
        _wait_rows(xn_hbm, xbuf.at[pslot], sem_x.at[pslot], to_hbm=False)
        gated_update(gate_cur)
        _start_rows_loop(y_hbm, idx_cur, sbuf, tm, 1, sem_s, to_hbm=True)
        _wait_rows(y_hbm, sbuf, sem_s, to_hbm=True)


def _experts(idx_tiles, gate_tiles, wg, wu, wd, xn_all, y_all):
    ne, d, dff = wg.shape
    tm, tf = EXPERT_TM, EXPERT_TF
    n_tiles = idx_tiles.shape[0]
    per_expert = n_tiles // ne
    n_steps = dff // tf
    xr = d // (2 * LANES)
    n_tok = xn_all.shape[0] // xr
    assert tm % (4 * n_steps) == 0 and d == dff and y_all.shape[0] >= n_tok + tm
    rows = jnp.arange(tm, dtype=jnp.int32)[None]
    idx_ext = jnp.concatenate([rows + n_tok, idx_tiles, rows], axis=0)
    gate_ext = jnp.concatenate([jnp.zeros((1, tm), jnp.float32), gate_tiles], axis=0)
    idx_ext = idx_ext.reshape(n_tiles + 2, 1, tm)
    gate_ext = gate_ext.reshape(n_tiles + 1, 1, tm)
    any_spec = pl.BlockSpec(memory_space=pl.ANY)
    idx_spec = lambda k: pl.BlockSpec((1, 1, tm), lambda i, f: (i + k, 0, 0), memory_space=pltpu.SMEM)
    gate_spec = lambda k: pl.BlockSpec((1, 1, tm), lambda i, f: (i + k, 0, 0))
    return pl.pallas_call(
        _expert_kernel,
        grid=(n_tiles, n_steps),
        in_specs=[
            idx_spec(0), idx_spec(1), idx_spec(2), gate_spec(0), gate_spec(1),
            pl.BlockSpec((None, d, tf), lambda i, f: (i // per_expert, 0, f)),
            pl.BlockSpec((None, d, tf), lambda i, f: (i // per_expert, 0, f)),
            pl.BlockSpec((None, tf, d), lambda i, f: (i // per_expert, f, 0)),
            any_spec, any_spec,
        ],
        out_specs=any_spec,
        out_shape=jax.ShapeDtypeStruct(y_all.shape, y_all.dtype),
        input_output_aliases={9: 0},
        scratch_shapes=[
            pltpu.VMEM((2, tm * xr, LANES), xn_all.dtype),
            pltpu.VMEM((tm, d), jnp.bfloat16),
            pltpu.VMEM((3, tm, d), jnp.float32),
            pltpu.VMEM((tm, LANES), jnp.float32),
            pltpu.SemaphoreType.DMA((2,)), pltpu.SemaphoreType.DMA, pltpu.SemaphoreType.DMA,
        ],
        compiler_params=pltpu.CompilerParams(
            dimension_semantics=("arbitrary", "arbitrary"), vmem_limit_bytes=VMEM_LIMIT),
        name="experts",
    )(idx_ext, idx_ext, idx_ext, gate_ext, gate_ext, wg, wu, wd, xn_all, y_all)


def _final_kernel(y_ref, g_ref, o_ref):
    o_ref[...] = _rms(y_ref[...], g_ref[...])


def _final_norm(y_all, g, row_offset, t):
    d = g.shape[1]
    tm = 512
    ob = row_offset // tm
    assert row_offset % tm == 0 and t % tm == 0
    return pl.pallas_call(
        _final_kernel,
        grid=(t // tm,),
        in_specs=[pl.BlockSpec((tm, d), lambda i: (i + ob, 0)),
                  pl.BlockSpec((1, d), lambda i: (0, 0))],
        out_specs=pl.BlockSpec((tm, d), lambda i: (i, 0)),
        out_shape=jax.ShapeDtypeStruct((t, d), jnp.float32),
        compiler_params=pltpu.CompilerParams(dimension_semantics=("arbitrary",)),
        name="final_norm",
    )(y_all, g)


def kernel(x_prompt, x_sample, norm1_g, w_in, v_norm_g, w_spatial, b_spatial, w_pool,
           b_pool, pool_scale, w_out, norm2_g, w_router, w_gate, w_up, w_down, final_norm_g):
    assert norm1_g.shape[0] == 1, "single-layer block"
    bf = jnp.bfloat16
    row = lambda a: a.reshape(1, -1)
    mixer_w = (row(norm1_g[0]), w_in[0].astype(bf), row(v_norm_g[0]), w_spatial[0].astype(bf),
               b_spatial[0].T, w_pool[0].astype(bf), b_pool[0], row(pool_scale[0]),
               w_out[0].astype(bf), row(norm2_g[0]), w_router[0].T.astype(bf))

    t_p = x_prompt.shape[0] * x_prompt.shape[1]
    t_s = x_sample.shape[0] * x_sample.shape[1]
    y_all, xn_all, aff = _mixer(x_prompt, x_sample, mixer_w, pad_rows=EXPERT_TM)
    idx_p, gate_p = _route(aff[:, :t_p], 0)
    idx_s, gate_s = _route(aff[:, t_p:], t_p)

    tiles = lambda a: jnp.concatenate([a[0], a[1]], axis=1).reshape(-1, EXPERT_TM)
    assert idx_p.shape[1] % EXPERT_TM == 0 and idx_s.shape[1] % EXPERT_TM == 0
    y_all = _experts(tiles((idx_p, idx_s)), tiles((gate_p, gate_s)),
                     w_gate[0], w_up[0], w_down[0], xn_all, y_all)

    fg = row(final_norm_g)
    return (_final_norm(y_all, fg, 0, t_p).reshape(x_prompt.shape),
            _final_norm(y_all, fg, t_p, t_s).reshape(x_sample.shape))
```
